```python
import functools
import jax, jax.numpy as jnp
from jax import lax
import numpy as np

D_MODEL = 1024
BATCH = 4
SEQ = 4096
DEPTH = 4
DEC_BATCH = 128
DEC_SEQ = 4
PAST_LEN = 2048
PAGE_SIZE = 128

N_MEM = 256
ATT_HEADS = 8
HEAD_DIM = 64
ATT_WIDTH = ATT_HEADS * HEAD_DIM
MOBA_BLOCK = 256
MOBA_TOPK = 3
Q_BLOCK = 64
SGU_GROUPS = 4
SGU_GROUP_DIM = 128
SGU_CHUNK = 128
SGU_WIDTH = SGU_GROUPS * SGU_GROUP_DIM
MIX_WIDTH = ATT_WIDTH + SGU_WIDTH
IN_WIDTH = 3 * ATT_WIDTH + 2 * SGU_WIDTH
X_HEADS = 4
X_HEAD_DIM = 64
X_WIDTH = X_HEADS * X_HEAD_DIM
D_FF = 2816
EPS = 1e-6
NEG = -1e30

kernel_name = 'moba_sgu_macaron_hybrid_step'


def rms_norm(x, g):
    xf = x.astype(jnp.float32)
    y = xf * lax.rsqrt(jnp.mean(xf * xf, axis=-1, keepdims=True) + EPS)
    return (y * g.astype(jnp.float32)).astype(x.dtype)


def swiglu(h, w_in_ff, w_out_ff):
    a, b = jnp.split(h @ w_in_ff, 2, axis=-1)
    return (jax.nn.silu(a) * b) @ w_out_ff


def alibi_slopes(n_heads):
    return jnp.asarray(2.0 ** (-8.0 * np.arange(1, n_heads + 1) / n_heads), dtype=jnp.float32)


def moba_core(q, qpos, kb, vb, kmean, k_own, v_own, own_start, slopes):
    B, Q, H, Dh = q.shape
    scale = Dh ** -0.5
    Lo = k_own.shape[2]
    own_pos = own_start + jnp.arange(Lo, dtype=jnp.int32)
    d_own = qpos[:, None] - own_pos[None, :]
    s_own = jnp.einsum('bqhd,bhsd->bqhs', q, k_own, preferred_element_type=jnp.float32) * scale
    s_own = s_own - slopes[None, None, :, None] * d_own[None, :, None, :].astype(jnp.float32)
    s_own = jnp.where((d_own >= 0)[None, :, None, :], s_own, NEG)
    n_blocks = kb.shape[2]
    k_sel = min(MOBA_TOPK, n_blocks)
    if k_sel == 0:
        p_own = jax.nn.softmax(s_own, axis=-1).astype(v_own.dtype)
        return jnp.einsum('bqhs,bhsd->bqhd', p_own, v_own)
    n_past = qpos // MOBA_BLOCK
    gate = jnp.einsum('bqhd,bhnd->bqhn', q, kmean, preferred_element_type=jnp.float32)
    gate = jnp.where(jnp.arange(n_blocks)[None, None, None, :] < n_past[None, :, None, None], gate, NEG)
    _, idx = lax.top_k(gate, k_sel)
    valid = jnp.arange(k_sel)[None, :] < n_past[:, None]
    bi = jnp.arange(B)[:, None, None, None]
    hi = jnp.arange(H)[None, None, :, None]
    k_g = kb[bi, hi, idx]
    v_g = vb[bi, hi, idx]
    key_pos = idx[..., None] * MOBA_BLOCK + jnp.arange(MOBA_BLOCK, dtype=jnp.int32)
    d_sel = (qpos[None, :, None, None, None] - key_pos).astype(jnp.float32)
    s_sel = jnp.einsum('bqhd,bqhjsd->bqhjs', q, k_g, preferred_element_type=jnp.float32) * scale
    s_sel = s_sel - slopes[None, None, :, None, None] * d_sel
    s_sel = jnp.where(valid[None, :, None, :, None], s_sel, NEG)
    n_sel = k_sel * MOBA_BLOCK
    s_all = jnp.concatenate([s_sel.reshape(B, Q, H, n_sel), s_own], axis=-1)
    p = jax.nn.softmax(s_all, axis=-1).astype(v_own.dtype)
    out = jnp.einsum('bqhjs,bqhjsd->bqhd', p[..., :n_sel].reshape(B, Q, H, k_sel, MOBA_BLOCK), v_g)
    return out + jnp.einsum('bqhs,bhsd->bqhd', p[..., n_sel:], v_own)


def moba_prompt(q, k, v, slopes):
    B, T, H, Dh = q.shape
    nb = -(-T // MOBA_BLOCK)
    pad = nb * MOBA_BLOCK - T

    def to_blocks(a):
        a = jnp.pad(a, ((0, 0), (0, pad), (0, 0), (0, 0)))
        return a.reshape(B, nb, MOBA_BLOCK, H, Dh).transpose(0, 3, 1, 2, 4)

    kb, vb = to_blocks(k), to_blocks(v)
    kmean = jnp.mean(kb, axis=3)
    n_qb = T // Q_BLOCK
    q_blocks = q.reshape(B, n_qb, Q_BLOCK, H, Dh).transpose(1, 0, 2, 3, 4)

    def step(args):
        i, q_i = args
        start = i * Q_BLOCK
        own = start // MOBA_BLOCK
        k_own = lax.dynamic_index_in_dim(kb, own, axis=2, keepdims=False)
        v_own = lax.dynamic_index_in_dim(vb, own, axis=2, keepdims=False)
        qpos = start + jnp.arange(Q_BLOCK, dtype=jnp.int32)
        return moba_core(q_i, qpos, kb, vb, kmean, k_own, v_own, own * MOBA_BLOCK, slopes)

    out = lax.map(step, (jnp.arange(n_qb, dtype=jnp.int32), q_blocks))
    return out.transpose(1, 0, 2, 3, 4).reshape(B, T, H, Dh)


def moba_sample(q, k, v, k_past, v_past, slopes):
    B, S, H, Dh = q.shape
    P = k_past.shape[1]
    nb = P // MOBA_BLOCK
    own_start = nb * MOBA_BLOCK
    k_all = jnp.concatenate([k_past, k], axis=1)
    v_all = jnp.concatenate([v_past, v], axis=1)

    def to_blocks(a):
        return a[:, :own_start].reshape(B, nb, MOBA_BLOCK, H, Dh).transpose(0, 3, 1, 2, 4)

    kb, vb = to_blocks(k_all), to_blocks(v_all)
    kmean = jnp.mean(kb, axis=3)
    k_own = k_all[:, own_start:].transpose(0, 2, 1, 3)
    v_own = v_all[:, own_start:].transpose(0, 2, 1, 3)
    q_rows = q.transpose(1, 0, 2, 3)[:, :, None]

    def step(args):
        j, q_j = args
        return moba_core(q_j, (P + j)[None], kb, vb, kmean, k_own, v_own, own_start, slopes)

    out = lax.map(step, (jnp.arange(S, dtype=jnp.int32), q_rows))
    return out[:, :, 0].transpose(1, 0, 2, 3)


def group_rms(v, g):
    lead = v.shape[:-1]
    vr = v.reshape(lead + (SGU_GROUPS, SGU_GROUP_DIM)).astype(jnp.float32)
    vr = vr * lax.rsqrt(jnp.mean(vr * vr, axis=-1, keepdims=True) + EPS)
    return (vr * g.reshape(SGU_GROUPS, SGU_GROUP_DIM).astype(jnp.float32)).astype(v.dtype)


def spatial_gating(u, vg, w_s, b_s, g_sgu):
    B, T, _ = u.shape
    L = min(T, SGU_CHUNK)
    nc = T // L
    v = group_rms(vg, g_sgu)
    mask = jnp.tril(jnp.ones((L, L), dtype=bool))
    ws = jnp.where(mask[None], w_s[:, :L, :L], 0.0).astype(v.dtype)
    vc = v.reshape(B, nc, L, SGU_GROUPS, SGU_GROUP_DIM)
    mixed = jnp.einsum('gts,bcsgd->bctgd', ws, vc) + b_s[:, :L].T[None, None, :, :, None]
    out = u.reshape(B, nc, L, SGU_GROUPS, SGU_GROUP_DIM) * mixed
    return out.reshape(B, T, SGU_WIDTH), v.reshape(B, T, SGU_WIDTH)


def memory_cross_attention(h, w_xq, w_xo, mem_k, mem_v):
    B, T, _ = h.shape
    q = (h @ w_xq).reshape(B, T, X_HEADS, X_HEAD_DIM)
    s = jnp.einsum('bthd,bmhd->bhtm', q, mem_k, preferred_element_type=jnp.float32) * X_HEAD_DIM ** -0.5
    p = jax.nn.softmax(s, axis=-1).astype(mem_v.dtype)
    o = jnp.einsum('bhtm,bmhd->bthd', p, mem_v)
    return o.reshape(B, T, X_WIDTH) @ w_xo


def memory_kv(mem, g_mem, w_xk, w_xv):
    B = mem.shape[0]
    hm = rms_norm(mem, g_mem)
    mk = (hm @ w_xk).reshape(B, N_MEM, X_HEADS, X_HEAD_DIM)
    mv = (hm @ w_xv).reshape(B, N_MEM, X_HEADS, X_HEAD_DIM)
    return mk, mv


def trunk_layer(x, w, mem_k, mem_v, attend):
    (g_ffa, w_ffa_in, w_ffa_out, g_mix, w_in, w_out, w_s, b_s, g_sgu,
     g_x, w_xq, w_xo, g_ffb, w_ffb_in, w_ffb_out) = w
    B, T, _ = x.shape
    x = x + 0.5 * swiglu(rms_norm(x, g_ffa), w_ffa_in, w_ffa_out)
    z = rms_norm(x, g_mix) @ w_in
    q, k, v, u, vg = jnp.split(z, [ATT_WIDTH, 2 * ATT_WIDTH, 3 * ATT_WIDTH, 3 * ATT_WIDTH + SGU_WIDTH], axis=-1)
    q = q.reshape(B, T, ATT_HEADS, HEAD_DIM)
    k = k.reshape(B, T, ATT_HEADS, HEAD_DIM)
    v = v.reshape(B, T, ATT_HEADS, HEAD_DIM)
    o_att = attend(q, k, v).reshape(B, T, ATT_WIDTH)
    o_sgu, v_rows = spatial_gating(jax.nn.gelu(u), jax.nn.gelu(vg), w_s, b_s, g_sgu)
    x = x + jnp.concatenate([o_att, o_sgu], axis=-1) @ w_out
    x = x + memory_cross_attention(rms_norm(x, g_x), w_xq, w_xo, mem_k, mem_v)
    x = x + 0.5 * swiglu(rms_norm(x, g_ffb), w_ffb_in, w_ffb_out)
    return x, k, v, v_rows


def setup_inputs(seed: int = 0) -> dict:
    key = jax.random.key(seed)
    ks = iter(jax.random.split(key, 40))
    n_pages = PAST_LEN // PAGE_SIZE
    n_used = DEC_BATCH * n_pages
    n_phys = (5 * n_used + 3) // 4

    def normal(shape, scale):
        return scale * jax.random.normal(next(ks), shape, dtype=jnp.float32)

    def gain(shape):
        return 1.0 + 0.02 * jax.random.normal(next(ks), shape, dtype=jnp.float32)

    return {
        'x_prompt': normal((BATCH, SEQ, D_MODEL), 1.0),
        'x_sample': normal((DEC_BATCH, DEC_SEQ, D_MODEL), 1.0),
        'cache_k': normal((DEPTH, n_phys, PAGE_SIZE, ATT_HEADS, HEAD_DIM), 1.0),
        'cache_v': normal((DEPTH, n_phys, PAGE_SIZE, ATT_HEADS, HEAD_DIM), 1.0),
        'cache_mem_k': normal((DEPTH, DEC_BATCH, N_MEM, X_HEADS, X_HEAD_DIM), 1.0),
        'cache_mem_v': normal((DEPTH, DEC_BATCH, N_MEM, X_HEADS, X_HEAD_DIM), 1.0),
        'page_table': jax.random.permutation(next(ks), n_phys)[:n_used].reshape(DEC_BATCH, n_pages).astype(jnp.int32),
        'mem_prompt': normal((BATCH, N_MEM, D_MODEL), 1.0),
        'g_ffa': gain((DEPTH, D_MODEL)),
        'w_ffa_in': normal((DEPTH, D_MODEL, 2 * D_FF), D_MODEL ** -0.5),
        'w_ffa_out': normal((DEPTH, D_FF, D_MODEL), D_FF ** -0.5),
        'g_mix': gain((DEPTH, D_MODEL)),
        'w_in': normal((DEPTH, D_MODEL, IN_WIDTH), D_MODEL ** -0.5),
        'w_out': normal((DEPTH, MIX_WIDTH, D_MODEL), MIX_WIDTH ** -0.5),
        'w_sgu_s': normal((DEPTH, SGU_GROUPS, SGU_CHUNK, SGU_CHUNK), SGU_CHUNK ** -0.5),
        'b_sgu_s': gain((DEPTH, SGU_GROUPS, SGU_CHUNK)),
        'g_sgu': gain((DEPTH, SGU_WIDTH)),
        'g_x': gain((DEPTH, D_MODEL)),
        'w_xq': normal((DEPTH, D_MODEL, X_WIDTH), D_MODEL ** -0.5),
        'g_mem': gain((DEPTH, D_MODEL)),
        'w_xk': normal((DEPTH, D_MODEL, X_WIDTH), D_MODEL ** -0.5),
        'w_xv': normal((DEPTH, D_MODEL, X_WIDTH), D_MODEL ** -0.5),
        'w_xo': normal((DEPTH, X_WIDTH, D_MODEL), X_WIDTH ** -0.5),
        'g_ffb': gain((DEPTH, D_MODEL)),
        'w_ffb_in': normal((DEPTH, D_MODEL, 2 * D_FF), D_MODEL ** -0.5),
        'w_ffb_out': normal((DEPTH, D_FF, D_MODEL), D_FF ** -0.5),
        'g_final': gain((D_MODEL,)),
    }


def reference(x_prompt, x_sample, cache_k, cache_v, cache_mem_k, cache_mem_v, page_table, mem_prompt,
              g_ffa, w_ffa_in, w_ffa_out, g_mix, w_in, w_out, w_sgu_s, b_sgu_s, g_sgu,
              g_x, w_xq, g_mem, w_xk, w_xv, w_xo, g_ffb, w_ffb_in, w_ffb_out, g_final):
    slopes = alibi_slopes(ATT_HEADS)
    xp, xs = x_prompt, x_sample
    Bp, T, _ = xp.shape
    Bs, S, _ = xs.shape
    past = page_table.shape[1] * PAGE_SIZE
    kp_l, vp_l, mkp_l, mvp_l, ks_l, vs_l, vrow_l = [], [], [], [], [], [], []
    for l in range(DEPTH):
        w = (g_ffa[l], w_ffa_in[l], w_ffa_out[l], g_mix[l], w_in[l], w_out[l], w_sgu_s[l], b_sgu_s[l], g_sgu[l],
             g_x[l], w_xq[l], w_xo[l], g_ffb[l], w_ffb_in[l], w_ffb_out[l])
        mk, mv = memory_kv(mem_prompt, g_mem[l], w_xk[l], w_xv[l])
        xp, kp, vp, _ = trunk_layer(xp, w, mk, mv, functools.partial(moba_prompt, slopes=slopes))
        kp_l.append(kp)
        vp_l.append(vp)
        mkp_l.append(mk)
        mvp_l.append(mv)
        k_past = cache_k[l, page_table].reshape(Bs, past, ATT_HEADS, HEAD_DIM)
        v_past = cache_v[l, page_table].reshape(Bs, past, ATT_HEADS, HEAD_DIM)
        attend_s = functools.partial(moba_sample, k_past=k_past, v_past=v_past, slopes=slopes)
        xs, ks, vs, vrow = trunk_layer(xs, w, cache_mem_k[l], cache_mem_v[l], attend_s)
        ks_l.append(ks)
        vs_l.append(vs)
        vrow_l.append(vrow)
    y_prompt = rms_norm(xp, g_final)
    y_sample = rms_norm(xs, g_final)
    k_prompt = jnp.stack(kp_l).reshape(DEPTH, Bp, T // PAGE_SIZE, PAGE_SIZE, ATT_HEADS, HEAD_DIM)
    v_prompt = jnp.stack(vp_l).reshape(DEPTH, Bp, T // PAGE_SIZE, PAGE_SIZE, ATT_HEADS, HEAD_DIM)
    mem_k_prompt = jnp.stack(mkp_l)
    mem_v_prompt = jnp.stack(mvp_l)
    k_sample = jnp.stack(ks_l)
    v_sample = jnp.stack(vs_l)
    sgu_v_sample = jnp.stack(vrow_l)
    return (y_prompt, y_sample, k_prompt, v_prompt, mem_k_prompt, mem_v_prompt, k_sample, v_sample, sgu_v_sample)
```

```python
import functools

import numpy as np
import jax
import jax.numpy as jnp
from jax import lax
from jax.experimental import pallas as pl
from jax.experimental.pallas import tpu as pltpu

F32 = jnp.float32
BF16 = jnp.bfloat16

EPS = 1e-6
NEG = -1e30

ATT_HEADS = 8
HEAD_DIM = 64
ATT_WIDTH = ATT_HEADS * HEAD_DIM
MOBA_BLOCK = 256
MOBA_TOPK = 3
SGU_GROUPS = 4
SGU_GROUP_DIM = 128
SGU_CHUNK = 128
SGU_WIDTH = SGU_GROUPS * SGU_GROUP_DIM
X_HEADS = 4
X_HEAD_DIM = 64
X_WIDTH = X_HEADS * X_HEAD_DIM
PAGE_SIZE = 128

V7X_LANES = 128
V7X_VMEM_BYTES = 64 * 1024 * 1024
VMEM_LIMIT_BYTES = V7X_VMEM_BYTES - 8 * 1024 * 1024

ROW_TILE = 512
FF_CHUNK = 256

_ALIBI_SLOPES = tuple(float(2.0 ** (-8.0 * (h + 1) / ATT_HEADS)) for h in range(ATT_HEADS))
_HIGHEST = lax.Precision.HIGHEST


def _params(*semantics):
    return pltpu.CompilerParams(dimension_semantics=semantics, vmem_limit_bytes=VMEM_LIMIT_BYTES)


def _resident(block, index_map):
    return pl.BlockSpec(block, index_map, pipeline_mode=pl.Buffered(1))


def _rms(x, g):
    return x * lax.rsqrt(jnp.mean(x * x, axis=-1, keepdims=True) + EPS) * g


def _dot(a, b):
    return jnp.dot(a, b, preferred_element_type=F32)


def _dot_nt(a, b, precision=None):
    return lax.dot_general(a, b, (((1,), (1,)), ((), ())), precision=precision, preferred_element_type=F32)


def _top_mask(gate, axis, n_valid, idx):
    n = gate.shape[axis]
    rank = jnp.zeros(gate.shape, jnp.int32)
    for other in range(n):
        g_o = lax.slice_in_dim(gate, other, other + 1, axis=axis)
        beats = (g_o > gate) | ((g_o == gate) & (other < idx))
        rank = rank + jnp.where(beats, jnp.where(other < n_valid, 1, 0), 0)
    return (rank < MOBA_TOPK) & (idx < n_valid)


def _ffn_kernel(*refs, pre_add, final_norm):
    it = iter(refs)
    x_ref = next(it)
    if pre_add:
        ox_ref, wxo_ref = next(it), next(it)
    g_ref, wa_ref, wb_ref, wo_ref = next(it), next(it), next(it), next(it)
    if final_norm:
        gf_ref = next(it)
    o_ref, s_ref = next(it), next(it)

    x = x_ref[...]
    if pre_add:
        x = x + _dot(ox_ref[...].astype(BF16), wxo_ref[...])
    h = _rms(x, g_ref[...]).astype(BF16)
    d_ff = wa_ref.shape[1]
    for c in range(d_ff // FF_CHUNK):
        cols = slice(c * FF_CHUNK, (c + 1) * FF_CHUNK)
        a = _dot(h, wa_ref[:, cols])
        b = _dot(h, wb_ref[:, cols])
        s_ref[:, cols] = (jax.nn.silu(a) * b).astype(BF16)
    y = x + 0.5 * _dot(s_ref[...], wo_ref[...])
    if final_norm:
        y = _rms(y, gf_ref[...])
    o_ref[...] = y


def _ffn(x, layer, g, w_in, w_out, *, pre=None, final_g=None):
    n, d = x.shape
    d_ff = w_out.shape[1]
    assert d_ff % FF_CHUNK == 0 and n % ROW_TILE == 0
    tm = ROW_TILE
    row = lambda i: (i, 0)
    args, specs = [x], [pl.BlockSpec((tm, d), row)]
    if pre is not None:
        ox, w_xo = pre
        args += [ox, w_xo]
        specs += [pl.BlockSpec((tm, ox.shape[1]), row), _resident((None,) + w_xo.shape[1:], lambda i: (layer, 0, 0))]
    args += [g, w_in, w_in, w_out]
    specs += [
        _resident((None, 1, d), lambda i: (layer, 0, 0)),
        _resident((None, d, d_ff), lambda i: (layer, 0, 0)),
        _resident((None, d, d_ff), lambda i: (layer, 0, 1)),
        _resident((None, d_ff, d), lambda i: (layer, 0, 0)),
    ]
    if final_g is not None:
        args.append(final_g)
        specs.append(_resident((1, d), lambda i: (0, 0)))
    return pl.pallas_call(
        functools.partial(_ffn_kernel, pre_add=pre is not None, final_norm=final_g is not None),
        out_shape=jax.ShapeDtypeStruct((n, d), F32),
        grid=(n // tm,),
        in_specs=specs,
        out_specs=pl.BlockSpec((tm, d), row),
        scratch_shapes=[pltpu.VMEM((tm, d_ff), BF16)],
        compiler_params=_params("parallel"),
        name="ffn",
    )(*args)


def _inproj_kernel(x_ref, g_ref, w_ref, ws_ref, bias_ref, gs_ref, *outs, prompt, chunk_len):
    tm = x_ref.shape[0]
    h = _rms(x_ref[...], g_ref[...]).astype(BF16)
    aw = ATT_WIDTH
    zq = _dot(h, w_ref[:, 0:aw]) * (HEAD_DIM ** -0.5)
    zk = _dot(h, w_ref[:, aw:2 * aw])
    zv = _dot(h, w_ref[:, 2 * aw:3 * aw])
    u = jax.nn.gelu(_dot(h, w_ref[:, 3 * aw:3 * aw + SGU_WIDTH]))
    vg = jax.nn.gelu(_dot(h, w_ref[:, 3 * aw + SGU_WIDTH:]))

    if prompt:
        qt_ref, k_ref, v_ref, kb_ref, vt_ref, km_ref, osgu_ref = outs
        qt_ref[...] = zq.T
        kb_ref[...] = zk.astype(BF16)
        vt_ref[...] = zv.T.astype(BF16)
        km_ref[...] = jnp.mean(zk.reshape(tm // MOBA_BLOCK, MOBA_BLOCK, aw), axis=1)
    else:
        q_ref, k_ref, v_ref, osgu_ref, vn_ref = outs
        q_ref[...] = zq
    k_ref[...] = zk
    v_ref[...] = zv

    r = lax.broadcasted_iota(jnp.int32, (SGU_CHUNK, SGU_CHUNK), 0)
    c = lax.broadcasted_iota(jnp.int32, (SGU_CHUNK, SGU_CHUNK), 1)
    mix_mask = (r // chunk_len == c // chunk_len) & (c <= r)
    for grp in range(SGU_GROUPS):
        cols = slice(grp * SGU_GROUP_DIM, (grp + 1) * SGU_GROUP_DIM)
        vgg = vg[:, cols]
        vn = vgg * lax.rsqrt(jnp.mean(vgg * vgg, axis=-1, keepdims=True) + EPS) * gs_ref[:, cols]
        if not prompt:
            vn_ref[:, cols] = vn
        vnb = vn.astype(BF16)
        wsm = jnp.where(mix_mask, ws_ref[grp], 0.0).astype(BF16)
        for ch in range(tm // SGU_CHUNK):
            rows = slice(ch * SGU_CHUNK, (ch + 1) * SGU_CHUNK)
            mixed = _dot(wsm, vnb[rows]) + bias_ref[:, cols]
            osgu_ref[rows, cols] = (u[rows, cols] * mixed).astype(BF16)


def _inproj(x, layer, g, w_in, ws_tiled, bias_rows, g_sgu, *, prompt, chunk_len, batch):
    n, d = x.shape
    tm = ROW_TILE
    assert n % tm == 0 and tm % MOBA_BLOCK == 0 and tm % SGU_CHUNK == 0
    row = lambda i: (i, 0)
    aw = ATT_WIDTH
    in_specs = [
        pl.BlockSpec((tm, d), row),
        _resident((None, 1, d), lambda i: (layer, 0, 0)),
        _resident((None, d, w_in.shape[2]), lambda i: (layer, 0, 0)),
        _resident((None, SGU_GROUPS, SGU_CHUNK, SGU_CHUNK), lambda i: (layer, 0, 0, 0)),
        _resident((None, SGU_CHUNK, SGU_WIDTH), lambda i: (layer, 0, 0)),
        _resident((None, 1, SGU_WIDTH), lambda i: (layer, 0, 0)),
    ]
    if prompt:
        t = n // batch
        tiles_per_seq = t // tm
        col = lambda i: (i // tiles_per_seq, 0, i % tiles_per_seq)
        out_shape = [
            jax.ShapeDtypeStruct((batch, aw, t), F32),
            jax.ShapeDtypeStruct((n, aw), F32),
            jax.ShapeDtypeStruct((n, aw), F32),
            jax.ShapeDtypeStruct((n, aw), BF16),
            jax.ShapeDtypeStruct((batch, aw, t), BF16),
            jax.ShapeDtypeStruct((n // tm, tm // MOBA_BLOCK, aw), F32),
            jax.ShapeDtypeStruct((n, SGU_WIDTH), BF16),
        ]
        out_specs = [
            pl.BlockSpec((None, aw, tm), col),
            pl.BlockSpec((tm, aw), row),
            pl.BlockSpec((tm, aw), row),
            pl.BlockSpec((tm, aw), row),
            pl.BlockSpec((None, aw, tm), col),
            pl.BlockSpec((None, tm // MOBA_BLOCK, aw), lambda i: (i, 0, 0)),
            pl.BlockSpec((tm, SGU_WIDTH), row),
        ]
    else:
        out_shape = [
            jax.ShapeDtypeStruct((n, aw), F32),
            jax.ShapeDtypeStruct((n, aw), F32),
            jax.ShapeDtypeStruct((n, aw), F32),
            jax.ShapeDtypeStruct((n, SGU_WIDTH), BF16),
            jax.ShapeDtypeStruct((n, SGU_WIDTH), F32),
        ]
        out_specs = [pl.BlockSpec((tm, aw), row)] * 3 + [pl.BlockSpec((tm, SGU_WIDTH), row)] * 2
    return pl.pallas_call(
        functools.partial(_inproj_kernel, prompt=prompt, chunk_len=chunk_len),
        out_shape=out_shape,
        grid=(n // tm,),
        in_specs=in_specs,
        out_specs=out_specs,
        compiler_params=_params("parallel"),
        name="inproj_prompt" if prompt else "inproj_sample",
    )(x, g, w_in, ws_tiled, bias_rows, g_sgu)


def _moba_prompt_kernel(qt_ref, kb_ref, vt_ref, km_ref, o_ref, bias_ref):
    blk = MOBA_BLOCK
    i = pl.program_id(1)
    n_blocks = km_ref.shape[0]
    pair_w = 2 * HEAD_DIM
    rk = lax.broadcasted_iota(jnp.int32, (blk, blk), 0)
    rq = lax.broadcasted_iota(jnp.int32, (blk, blk), 1)
    dist = (rq - rk).astype(F32)
    causal = rq >= rk
    pair_row = lax.broadcasted_iota(jnp.int32, (pair_w, blk), 0)
    blk_idx = lax.broadcasted_iota(jnp.int32, (n_blocks, blk), 0)
    own_start = pl.multiple_of(i * blk, blk)

    outs = []
    for h in range(ATT_HEADS):
        pair = slice((h // 2) * pair_w, (h // 2 + 1) * pair_w)
        slope = _ALIBI_SLOPES[h]
        qm = jnp.where(pair_row // HEAD_DIM == h % 2, qt_ref[pair, :], 0.0)
        qmb = qm.astype(BF16)
        gate = jnp.dot(km_ref[:, pair], qm, precision=_HIGHEST, preferred_element_type=F32)
        sel = _top_mask(gate, 0, i, blk_idx)
        bias_ref[h * n_blocks:(h + 1) * n_blocks, :] = jnp.where(
            sel, (-slope * blk) * (i - blk_idx).astype(F32), NEG)
        base = -slope * dist
        head_rows = slice(h * HEAD_DIM, (h + 1) * HEAD_DIM)

        s = _dot(kb_ref[pl.ds(own_start, blk), pair], qmb) + jnp.where(causal, base, NEG)
        m = jnp.max(s, axis=0, keepdims=True)
        p = jnp.exp(s - m)
        l = jnp.sum(p, axis=0, keepdims=True)
        acc = _dot(vt_ref[head_rows, pl.ds(own_start, blk)], p.astype(BF16))

        def body(j, carry, h=h, pair=pair, qmb=qmb, base=base, head_rows=head_rows):
            m, l, acc = carry
            start = pl.multiple_of(j * blk, blk)
            s = _dot(kb_ref[pl.ds(start, blk), pair], qmb) + base + bias_ref[pl.ds(h * n_blocks + j, 1), :]
            m_new = jnp.maximum(m, jnp.max(s, axis=0, keepdims=True))
            alpha = jnp.exp(m - m_new)
            p = jnp.exp(s - m_new)
            l = alpha * l + jnp.sum(p, axis=0, keepdims=True)
            acc = alpha * acc + _dot(vt_ref[head_rows, pl.ds(start, blk)], p.astype(BF16))
            return m_new, l, acc

        m, l, acc = lax.fori_loop(0, i, body, (m, l, acc))
        outs.append(acc / l)
    o_ref[...] = jnp.concatenate(outs, axis=0).T.astype(BF16)


def _moba_prompt(qt, kb, vt, kmean):
    batch, aw, t = qt.shape
    n_blocks = t // MOBA_BLOCK
    assert t % MOBA_BLOCK == 0 and kmean.shape == (batch, n_blocks, aw)
    return pl.pallas_call(
        _moba_prompt_kernel,
        out_shape=jax.ShapeDtypeStruct((batch, t, aw), BF16),
        grid=(batch, n_blocks),
        in_specs=[
            pl.BlockSpec((None, aw, MOBA_BLOCK), lambda b, i: (b, 0, i)),
            pl.BlockSpec((None, t, aw), lambda b, i: (b, 0, 0)),
            pl.BlockSpec((None, aw, t), lambda b, i: (b, 0, 0)),
            pl.BlockSpec((None, n_blocks, aw), lambda b, i: (b, 0, 0)),
        ],
        out_specs=pl.BlockSpec((None, MOBA_BLOCK, aw), lambda b, i: (b, i, 0)),
        scratch_shapes=[pltpu.VMEM((ATT_HEADS * n_blocks, MOBA_BLOCK), F32)],
        compiler_params=_params("parallel", "arbitrary"),
        name="moba_prompt",
    )(qt, kb, vt, kmean)


_ROWS_PER_QUERY = 8


def _block_diag_queries(q, head_dim):
    s_len, w = q.shape
    lane_head = lax.broadcasted_iota(jnp.int32, (_ROWS_PER_QUERY, w), 1) // head_dim
    row_head = lax.broadcasted_iota(jnp.int32, (_ROWS_PER_QUERY, w), 0)
    keep = lane_head == row_head
    parts = [jnp.where(keep, jnp.broadcast_to(q[s:s + 1, :], (_ROWS_PER_QUERY, w)), 0.0) for s in range(s_len)]
    return jnp.concatenate(parts, axis=0), keep


def _pick_block_diag(r, keep, s_len):
    w = r.shape[1]
    keep_all = jnp.concatenate([keep] * s_len, axis=0)
    return jnp.sum(jnp.where(keep_all, r, 0.0).reshape(s_len, _ROWS_PER_QUERY, w), axis=1)


def _moba_sample_kernel(pt_ref, q_ref, ko_ref, vo_ref, *rest, n_pages, past_len):
    del pt_ref
    k_pages, v_pages, o_ref = rest[:n_pages], rest[n_pages:2 * n_pages], rest[2 * n_pages]
    s_len, aw = q_ref.shape
    blk = MOBA_BLOCK
    pages_per_block = blk // PAGE_SIZE
    n_blocks = past_len // blk
    n_rows = s_len * _ROWS_PER_QUERY

    q = q_ref[...]
    qbd, keep = _block_diag_queries(q, HEAD_DIM)
    qbd_b = qbd.astype(BF16)
    row = lax.broadcasted_iota(jnp.int32, (n_rows, 1), 0)
    row_head = row % _ROWS_PER_QUERY
    slope = jnp.zeros((n_rows, 1), F32)
    for h in range(ATT_HEADS):
        slope = jnp.where(row_head == h, _ALIBI_SLOPES[h], slope)
    qpos = past_len + row // _ROWS_PER_QUERY

    scores, means = [], []
    for n in range(n_blocks):
        kblk = jnp.concatenate([k_pages[n * pages_per_block + t][...] for t in range(pages_per_block)], axis=0)
        means.append(jnp.mean(kblk, axis=0, keepdims=True))
        scores.append(_dot_nt(qbd_b, kblk.astype(BF16)))
    kmean = jnp.concatenate(means, axis=0)
    gate = _dot_nt(qbd, kmean, precision=_HIGHEST)
    sel = _top_mask(gate, 1, n_blocks, lax.broadcasted_iota(jnp.int32, (n_rows, n_blocks), 1))

    unselected = jnp.where(sel, 0.0, NEG)
    key_off = lax.broadcasted_iota(jnp.int32, (n_rows, blk), 1)
    for n in range(n_blocks):
        dist = (qpos - (n * blk + key_off)).astype(F32)
        scores[n] = scores[n] - slope * dist + unselected[:, n:n + 1]
    k_own, v_own = ko_ref[...], vo_ref[...]
    own = []
    for t in range(s_len):
        s_t = jnp.sum(qbd * k_own[t:t + 1, :], axis=1, keepdims=True)
        d_t = qpos - (past_len + t)
        own.append(jnp.where(d_t >= 0, s_t - slope * d_t.astype(F32), NEG))

    m = own[0]
    for s_t in own[1:]:
        m = jnp.maximum(m, s_t)
    for s_n in scores:
        m = jnp.maximum(m, jnp.max(s_n, axis=1, keepdims=True))
    l = jnp.zeros((n_rows, 1), F32)
    acc = jnp.zeros((n_rows, aw), F32)
    for t in range(s_len):
        p_t = jnp.exp(own[t] - m)
        l = l + p_t
        acc = acc + p_t * v_own[t:t + 1, :]
    for n in range(n_blocks):
        p_n = jnp.exp(scores[n] - m)
        l = l + jnp.sum(p_n, axis=1, keepdims=True)
        vblk = jnp.concatenate([v_pages[n * pages_per_block + t][...] for t in range(pages_per_block)], axis=0)
        acc = acc + _dot(p_n.astype(BF16), vblk.astype(BF16))
    o_ref[...] = _pick_block_diag(acc / l, keep, s_len)


def _moba_sample(page_table, q3, k3, v3, cache_k, cache_v, layer):
    n_req, s_len, aw = q3.shape
    n_pages = page_table.shape[1]
    past_len = n_pages * PAGE_SIZE
    assert past_len % MOBA_BLOCK == 0 and s_len <= MOBA_BLOCK and MOBA_BLOCK % PAGE_SIZE == 0
    assert past_len // MOBA_BLOCK >= MOBA_TOPK
    new = pl.BlockSpec((None, s_len, aw), lambda b, pt: (b, 0, 0))
    page_specs = [
        pl.BlockSpec((None, None, PAGE_SIZE, aw), lambda b, pt, p=p: (layer, pt[b, p], 0, 0)) for p in range(n_pages)
    ]
    return pl.pallas_call(
        functools.partial(_moba_sample_kernel, n_pages=n_pages, past_len=past_len),
        out_shape=jax.ShapeDtypeStruct((n_req, s_len, aw), F32),
        grid_spec=pltpu.PrefetchScalarGridSpec(
            num_scalar_prefetch=1,
            grid=(n_req,),
            in_specs=[new, new, new] + page_specs + page_specs,
            out_specs=pl.BlockSpec((None, s_len, aw), lambda b, pt: (b, 0, 0)),
        ),
        compiler_params=_params("parallel"),
        name="moba_sample",
    )(page_table, q3, k3, v3, *([cache_k] * n_pages), *([cache_v] * n_pages))


def _mix_out(x_ref, oatt_ref, osgu_ref, wo_ref):
    return (x_ref[...] + _dot(oatt_ref[...].astype(BF16), wo_ref[0:ATT_WIDTH, :])
            + _dot(osgu_ref[...], wo_ref[ATT_WIDTH:, :]))


def _outx_prompt_kernel(x_ref, oatt_ref, osgu_ref, wo_ref, gx_ref, wxq_ref, mk_ref, mv_ref, wxo_ref, o_ref):
    tm = x_ref.shape[0]
    x2 = _mix_out(x_ref, oatt_ref, osgu_ref, wo_ref)
    qx = _dot(_rms(x2, gx_ref[...]).astype(BF16), wxq_ref[...]) * (X_HEAD_DIM ** -0.5)
    mk = mk_ref[...].astype(BF16)
    mv = mv_ref[...].astype(BF16)
    pair_w = 2 * X_HEAD_DIM
    lane = lax.broadcasted_iota(jnp.int32, (tm, pair_w), 1)
    pieces = []
    for hp in range(X_HEADS // 2):
        pair = slice(hp * pair_w, (hp + 1) * pair_w)
        q_pair = qx[:, pair]
        res = []
        for sub in range(2):
            qm = jnp.where(lane // X_HEAD_DIM == sub, q_pair, 0.0).astype(BF16)
            s = _dot_nt(qm, mk[:, pair])
            p = jnp.exp(s - jnp.max(s, axis=1, keepdims=True))
            p = p / jnp.sum(p, axis=1, keepdims=True)
            res.append(_dot(p.astype(BF16), mv[:, pair]))
        pieces.append(jnp.where(lane // X_HEAD_DIM == 0, res[0], res[1]))
    ox = jnp.concatenate(pieces, axis=1).astype(BF16)
    o_ref[...] = x2 + _dot(ox, wxo_ref[...])


def _outx_prompt(x, oatt, osgu, layer, w_out, g_x, w_xq, mem_k, mem_v, w_xo, *, batch):
    n, d = x.shape
    tm = ROW_TILE
    tiles_per_seq = n // batch // tm
    n_mem = mem_k.shape[2]
    row = lambda i: (i, 0)
    mem = pl.BlockSpec((None, None, n_mem, X_WIDTH), lambda i: (layer, i // tiles_per_seq, 0, 0))
    return pl.pallas_call(
        _outx_prompt_kernel,
        out_shape=jax.ShapeDtypeStruct((n, d), F32),
        grid=(n // tm,),
        in_specs=[
            pl.BlockSpec((tm, d), row),
            pl.BlockSpec((tm, ATT_WIDTH), row),
            pl.BlockSpec((tm, SGU_WIDTH), row),
            _resident((None,) + w_out.shape[1:], lambda i: (layer, 0, 0)),
            _resident((None, 1, d), lambda i: (layer, 0, 0)),
            _resident((None, d, X_WIDTH), lambda i: (layer, 0, 0)),
            mem,
            mem,
            _resident((None, X_WIDTH, d), lambda i: (layer, 0, 0)),
        ],
        out_specs=pl.BlockSpec((tm, d), row),
        compiler_params=_params("parallel"),
        name="outx_prompt",
    )(x, oatt, osgu, w_out, g_x, w_xq, mem_k, mem_v, w_xo)


def _outq_sample_kernel(x_ref, oatt_ref, osgu_ref, wo_ref, gx_ref, wxq_ref, x2_ref, q_ref):
    x2 = _mix_out(x_ref, oatt_ref, osgu_ref, wo_ref)
    x2_ref[...] = x2
    q_ref[...] = _dot(_rms(x2, gx_ref[...]).astype(BF16), wxq_ref[...]) * (X_HEAD_DIM ** -0.5)


def _outq_sample(x, oatt, osgu, layer, w_out, g_x, w_xq):
    n, d = x.shape
    tm = ROW_TILE
    row = lambda i: (i, 0)
    return pl.pallas_call(
        _outq_sample_kernel,
        out_shape=[jax.ShapeDtypeStruct((n, d), F32), jax.ShapeDtypeStruct((n, X_WIDTH), F32)],
        grid=(n // tm,),
        in_specs=[
            pl.BlockSpec((tm, d), row),
            pl.BlockSpec((tm, ATT_WIDTH), row),
            pl.BlockSpec((tm, SGU_WIDTH), row),
            _resident((None,) + w_out.shape[1:], lambda i: (layer, 0, 0)),
            _resident((None, 1, d), lambda i: (layer, 0, 0)),
            _resident((None, d, X_WIDTH), lambda i: (layer, 0, 0)),
        ],
        out_specs=[pl.BlockSpec((tm, d), row), pl.BlockSpec((tm, X_WIDTH), row)],
        compiler_params=_params("parallel"),
        name="outq_sample",
    )(x, oatt, osgu, w_out, g_x, w_xq)


_XATT_REQ_TILE = 8


def _xatt_sample_kernel(q_ref, mk_ref, mv_ref, o_ref):
    s_len = q_ref.shape[1]
    for b in range(q_ref.shape[0]):
        qbd, keep = _block_diag_queries(q_ref[b], X_HEAD_DIM)
        s = _dot_nt(qbd.astype(BF16), mk_ref[b].astype(BF16))
        p = jnp.exp(s - jnp.max(s, axis=1, keepdims=True))
        p = p / jnp.sum(p, axis=1, keepdims=True)
        r = _dot(p.astype(BF16), mv_ref[b].astype(BF16))
        o_ref[b] = _pick_block_diag(r, keep, s_len)


def _xatt_sample(q3, mem_k, mem_v, layer):
    n_req, s_len, _ = q3.shape
    n_mem = mem_k.shape[2]
    bt = _XATT_REQ_TILE
    assert n_req % bt == 0
    mem = pl.BlockSpec((None, bt, n_mem, X_WIDTH), lambda i: (layer, i, 0, 0))
    blk = pl.BlockSpec((bt, s_len, X_WIDTH), lambda i: (i, 0, 0))
    return pl.pallas_call(
        _xatt_sample_kernel,
        out_shape=jax.ShapeDtypeStruct((n_req, s_len, X_WIDTH), F32),
        grid=(n_req // bt,),
        in_specs=[blk, mem, mem],
        out_specs=blk,
        compiler_params=_params("parallel"),
        name="xatt_sample",
    )(q3, mem_k, mem_v)


def _memkv_kernel(mem_ref, g_ref, wk_ref, wv_ref, mk_ref, mv_ref):
    hm = _rms(mem_ref[...], g_ref[...]).astype(BF16)
    mk_ref[...] = _dot(hm, wk_ref[...])
    mv_ref[...] = _dot(hm, wv_ref[...])


def _memkv(mem, g_mem, w_xk, w_xv):
    n, d = mem.shape
    depth = g_mem.shape[0]
    tm = min(ROW_TILE, n)
    assert n % tm == 0
    w = pl.BlockSpec((None, d, X_WIDTH), lambda l, i: (l, 0, 0))
    out = pl.BlockSpec((None, tm, X_WIDTH), lambda l, i: (l, i, 0))
    return pl.pallas_call(
        _memkv_kernel,
        out_shape=[jax.ShapeDtypeStruct((depth, n, X_WIDTH), F32)] * 2,
        grid=(depth, n // tm),
        in_specs=[pl.BlockSpec((tm, d), lambda l, i: (i, 0)), pl.BlockSpec((None, 1, d), lambda l, i: (l, 0, 0)), w, w],
        out_specs=[out, out],
        compiler_params=_params("parallel", "parallel"),
        name="memkv",
    )(mem, g_mem, w_xk, w_xv)


def kernel(x_prompt, x_sample, cache_k, cache_v, cache_mem_k, cache_mem_v, page_table, mem_prompt, g_ffa, w_ffa_in, w_ffa_out, g_mix, w_in, w_out, w_sgu_s, b_sgu_s, g_sgu, g_x, w_xq, g_mem, w_xk, w_xv, w_xo, g_ffb, w_ffb_in, w_ffb_out, g_final):
    bp, t, d = x_prompt.shape
    bs, s_len, _ = x_sample.shape
    depth = g_ffa.shape[0]
    n_mem = mem_prompt.shape[1]
    n_phys = cache_k.shape[1]
    assert t % SGU_CHUNK == 0 and SGU_CHUNK % s_len == 0 and s_len < SGU_CHUNK

    bf = lambda w: w.astype(BF16)
    w_ffa_in, w_ffa_out, w_ffb_in, w_ffb_out = bf(w_ffa_in), bf(w_ffa_out), bf(w_ffb_in), bf(w_ffb_out)
    w_in, w_out, w_xq, w_xk, w_xv, w_xo = bf(w_in), bf(w_out), bf(w_xq), bf(w_xk), bf(w_xv), bf(w_xo)
    g3 = lambda g: g.reshape(depth, 1, g.shape[-1])
    g_ffa, g_mix, g_sgu, g_x, g_mem, g_ffb = g3(g_ffa), g3(g_mix), g3(g_sgu), g3(g_x), g3(g_mem), g3(g_ffb)
    g_final = g_final.reshape(1, d)

    reps = SGU_CHUNK // s_len
    ws_prompt = w_sgu_s
    ws_sample = jnp.tile(w_sgu_s[:, :, :s_len, :s_len], (1, 1, reps, reps))
    bias_prompt = jnp.repeat(jnp.swapaxes(b_sgu_s, 1, 2), SGU_GROUP_DIM, axis=2)
    bias_sample = jnp.tile(bias_prompt[:, :s_len], (1, reps, 1))

    cache_k = cache_k.reshape(depth, n_phys, PAGE_SIZE, ATT_WIDTH)
    cache_v = cache_v.reshape(depth, n_phys, PAGE_SIZE, ATT_WIDTH)
    cache_mem_k = cache_mem_k.reshape(depth, bs, n_mem, X_WIDTH)
    cache_mem_v = cache_mem_v.reshape(depth, bs, n_mem, X_WIDTH)

    mk_p, mv_p = _memkv(mem_prompt.reshape(bp * n_mem, d), g_mem, w_xk, w_xv)
    mk_p4 = mk_p.reshape(depth, bp, n_mem, X_WIDTH)
    mv_p4 = mv_p.reshape(depth, bp, n_mem, X_WIDTH)

    xp = x_prompt.reshape(bp * t, d)
    xs = x_sample.reshape(bs * s_len, d)
    kp_l, vp_l, ks_l, vs_l, vrow_l = [], [], [], [], []
    for l in range(depth):
        last = l == depth - 1
        xp = _ffn(xp, l, g_ffa, w_ffa_in, w_ffa_out)
        qt, k, v, kb, vt, kmean, osgu = _inproj(
            xp, l, g_mix, w_in, ws_prompt, bias_prompt, g_sgu, prompt=True, chunk_len=SGU_CHUNK, batch=bp)
        kp_l.append(k)
        vp_l.append(v)
        oatt = _moba_prompt(qt, kb.reshape(bp, t, ATT_WIDTH), vt, kmean.reshape(bp, t // MOBA_BLOCK, ATT_WIDTH))
        xp = _outx_prompt(xp, oatt.reshape(bp * t, ATT_WIDTH), osgu, l, w_out, g_x, w_xq, mk_p4, mv_p4, w_xo, batch=bp)
        xp = _ffn(xp, l, g_ffb, w_ffb_in, w_ffb_out, final_g=g_final if last else None)
        xs = _ffn(xs, l, g_ffa, w_ffa_in, w_ffa_out)
        q, k, v, osgu, vn = _inproj(
            xs, l, g_mix, w_in, ws_sample, bias_sample, g_sgu, prompt=False, chunk_len=s_len, batch=bs)
        ks_l.append(k)
        vs_l.append(v)
        vrow_l.append(vn)
        to3 = lambda a: a.reshape(bs, s_len, a.shape[-1])
        oatt = _moba_sample(page_table, to3(q), to3(k), to3(v), cache_k, cache_v, l)
        x2, qx = _outq_sample(xs, oatt.reshape(bs * s_len, ATT_WIDTH), osgu, l, w_out, g_x, w_xq)
        ox = _xatt_sample(to3(qx), cache_mem_k, cache_mem_v, l)
        xs = _ffn(x2, l, g_ffb, w_ffb_in, w_ffb_out, pre=(ox.reshape(bs * s_len, X_WIDTH), w_xo),
                  final_g=g_final if last else None)

    y_prompt = xp.reshape(bp, t, d)
    y_sample = xs.reshape(bs, s_len, d)
    k_prompt = jnp.stack(kp_l).reshape(depth, bp, t // PAGE_SIZE, PAGE_SIZE, ATT_HEADS, HEAD_DIM)
    v_prompt = jnp.stack(vp_l).reshape(depth, bp, t // PAGE_SIZE, PAGE_SIZE, ATT_HEADS, HEAD_DIM)
    mem_k_prompt = mk_p.reshape(depth, bp, n_mem, X_HEADS, X_HEAD_DIM)
    mem_v_prompt = mv_p.reshape(depth, bp, n_mem, X_HEADS, X_HEAD_DIM)
    k_sample = jnp.stack(ks_l).reshape(depth, bs, s_len, ATT_HEADS, HEAD_DIM)
    v_sample = jnp.stack(vs_l).reshape(depth, bs, s_len, ATT_HEADS, HEAD_DIM)
    sgu_v_sample = jnp.stack(vrow_l).reshape(depth, bs, s_len, SGU_WIDTH)
    return (y_prompt, y_sample, k_prompt, v_prompt, mem_k_prompt, mem_v_prompt, k_sample, v_sample, sgu_v_sample)
```

```python
import functools

import numpy as np
import jax
import jax.numpy as jnp
from jax import lax
from jax.experimental import pallas as pl
from jax.experimental.pallas import tpu as pltpu

F32 = jnp.float32
BF16 = jnp.bfloat16

EPS = 1e-6
NEG = -1e30

ATT_HEADS = 8
HEAD_DIM = 64
ATT_WIDTH = ATT_HEADS * HEAD_DIM
MOBA_BLOCK = 256
MOBA_TOPK = 3
SGU_GROUPS = 4
SGU_GROUP_DIM = 128
SGU_CHUNK = 128
SGU_WIDTH = SGU_GROUPS * SGU_GROUP_DIM
X_HEADS = 4
X_HEAD_DIM = 64
X_WIDTH = X_HEADS * X_HEAD_DIM
PAGE_SIZE = 128

V7X_LANES = 128
V7X_VMEM_BYTES = 64 * 1024 * 1024
VMEM_LIMIT_BYTES = V7X_VMEM_BYTES - 8 * 1024 * 1024

ROW_TILE = 512
FF_CHUNK = 256

_ALIBI_SLOPES = tuple(float(2.0 ** (-8.0 * (h + 1) / ATT_HEADS)) for h in range(ATT_HEADS))
_HIGHEST = lax.Precision.HIGHEST
_LOG2E = float(np.log2(np.e))


def _params(*semantics):
    return pltpu.CompilerParams(dimension_semantics=semantics, vmem_limit_bytes=VMEM_LIMIT_BYTES)


def _resident(block, index_map):
    return pl.BlockSpec(block, index_map, pipeline_mode=pl.Buffered(1))


def _rms(x, g):
    return x * lax.rsqrt(jnp.mean(x * x, axis=-1, keepdims=True) + EPS) * g


def _dot(a, b):
    return jnp.dot(a, b, preferred_element_type=F32)


def _dot_nt(a, b, precision=None):
    return lax.dot_general(a, b, (((1,), (1,)), ((), ())), precision=precision, preferred_element_type=F32)


def _top_mask(gate, axis, n_valid, idx):
    n = gate.shape[axis]
    rank = jnp.zeros(gate.shape, jnp.int32)
    for other in range(n):
        g_o = lax.slice_in_dim(gate, other, other + 1, axis=axis)
        beats = (g_o > gate) | ((g_o == gate) & (other < idx))
        rank = rank + jnp.where(beats, jnp.where(other < n_valid, 1, 0), 0)
    return (rank < MOBA_TOPK) & (idx < n_valid)


def _ffn_kernel(*refs, pre_add, final_norm):
    it = iter(refs)
    x_ref = next(it)
    if pre_add:
        ox_ref, wxo_ref = next(it), next(it)
    g_ref, wa_ref, wb_ref, wo_ref = next(it), next(it), next(it), next(it)
    if final_norm:
        gf_ref = next(it)
    o_ref, s_ref = next(it), next(it)

    x = x_ref[...]
    if pre_add:
        x = x + _dot(ox_ref[...].astype(BF16), wxo_ref[...])
    h = _rms(x, g_ref[...]).astype(BF16)
    d_ff = wa_ref.shape[1]
    for c in range(d_ff // FF_CHUNK):
        cols = slice(c * FF_CHUNK, (c + 1) * FF_CHUNK)
        a = _dot(h, wa_ref[:, cols])
        b = _dot(h, wb_ref[:, cols])
        s_ref[:, cols] = (jax.nn.silu(a) * b).astype(BF16)
    y = x + 0.5 * _dot(s_ref[...], wo_ref[...])
    if final_norm:
        y = _rms(y, gf_ref[...])
    o_ref[...] = y


def _ffn(x, layer, g, w_in, w_out, *, pre=None, final_g=None):
    n, d = x.shape
    d_ff = w_out.shape[1]
    assert d_ff % FF_CHUNK == 0 and n % ROW_TILE == 0
    tm = ROW_TILE
    row = lambda i: (i, 0)
    args, specs = [x], [pl.BlockSpec((tm, d), row)]
    if pre is not None:
        ox, w_xo = pre
        args += [ox, w_xo]
        specs += [pl.BlockSpec((tm, ox.shape[1]), row), _resident((None,) + w_xo.shape[1:], lambda i: (layer, 0, 0))]
    args += [g, w_in, w_in, w_out]
    specs += [
        _resident((None, 1, d), lambda i: (layer, 0, 0)),
        _resident((None, d, d_ff), lambda i: (layer, 0, 0)),
        _resident((None, d, d_ff), lambda i: (layer, 0, 1)),
        _resident((None, d_ff, d), lambda i: (layer, 0, 0)),
    ]
    if final_g is not None:
        args.append(final_g)
        specs.append(_resident((1, d), lambda i: (0, 0)))
    return pl.pallas_call(
        functools.partial(_ffn_kernel, pre_add=pre is not None, final_norm=final_g is not None),
        out_shape=jax.ShapeDtypeStruct((n, d), F32),
        grid=(n // tm,),
        in_specs=specs,
        out_specs=pl.BlockSpec((tm, d), row),
        scratch_shapes=[pltpu.VMEM((tm, d_ff), BF16)],
        compiler_params=_params("parallel"),
        name="ffn",
    )(*args)


def _inproj_kernel(x_ref, g_ref, w_ref, ws_ref, bias_ref, gs_ref, *outs, prompt, chunk_len):
    tm = x_ref.shape[0]
    h = _rms(x_ref[...], g_ref[...]).astype(BF16)
    aw = ATT_WIDTH
    zq = _dot(h, w_ref[:, 0:aw]) * (HEAD_DIM ** -0.5)
    zk = _dot(h, w_ref[:, aw:2 * aw])
    zv = _dot(h, w_ref[:, 2 * aw:3 * aw])
    u = jax.nn.gelu(_dot(h, w_ref[:, 3 * aw:3 * aw + SGU_WIDTH]))
    vg = jax.nn.gelu(_dot(h, w_ref[:, 3 * aw + SGU_WIDTH:]))

    if prompt:
        qt_ref, k_ref, v_ref, kb_ref, vt_ref, km_ref, osgu_ref = outs
        qt_ref[...] = zq.T
        kb_ref[...] = zk.astype(BF16)
        vt_ref[...] = zv.T.astype(BF16)
        km_ref[...] = jnp.mean(zk.reshape(tm // MOBA_BLOCK, MOBA_BLOCK, aw), axis=1)
    else:
        q_ref, k_ref, v_ref, osgu_ref, vn_ref = outs
        q_ref[...] = zq
    k_ref[...] = zk
    v_ref[...] = zv

    r = lax.broadcasted_iota(jnp.int32, (SGU_CHUNK, SGU_CHUNK), 0)
    c = lax.broadcasted_iota(jnp.int32, (SGU_CHUNK, SGU_CHUNK), 1)
    mix_mask = (r // chunk_len == c // chunk_len) & (c <= r)
    for grp in range(SGU_GROUPS):
        cols = slice(grp * SGU_GROUP_DIM, (grp + 1) * SGU_GROUP_DIM)
        vgg = vg[:, cols]
        vn = vgg * lax.rsqrt(jnp.mean(vgg * vgg, axis=-1, keepdims=True) + EPS) * gs_ref[:, cols]
        if not prompt:
            vn_ref[:, cols] = vn
        vnb = vn.astype(BF16)
        wsm = jnp.where(mix_mask, ws_ref[grp], 0.0).astype(BF16)
        for ch in range(tm // SGU_CHUNK):
            rows = slice(ch * SGU_CHUNK, (ch + 1) * SGU_CHUNK)
            mixed = _dot(wsm, vnb[rows]) + bias_ref[:, cols]
            osgu_ref[rows, cols] = (u[rows, cols] * mixed).astype(BF16)


def _inproj(x, layer, g, w_in, ws_tiled, bias_rows, g_sgu, *, prompt, chunk_len, batch):
    n, d = x.shape
    tm = ROW_TILE
    assert n % tm == 0 and tm % MOBA_BLOCK == 0 and tm % SGU_CHUNK == 0
    row = lambda i: (i, 0)
    aw = ATT_WIDTH
    in_specs = [
        pl.BlockSpec((tm, d), row),
        _resident((None, 1, d), lambda i: (layer, 0, 0)),
        _resident((None, d, w_in.shape[2]), lambda i: (layer, 0, 0)),
        _resident((None, SGU_GROUPS, SGU_CHUNK, SGU_CHUNK), lambda i: (layer, 0, 0, 0)),
        _resident((None, SGU_CHUNK, SGU_WIDTH), lambda i: (layer, 0, 0)),
        _resident((None, 1, SGU_WIDTH), lambda i: (layer, 0, 0)),
    ]
    if prompt:
        t = n // batch
        tiles_per_seq = t // tm
        col = lambda i: (i // tiles_per_seq, 0, i % tiles_per_seq)
        out_shape = [
            jax.ShapeDtypeStruct((batch, aw, t), F32),
            jax.ShapeDtypeStruct((n, aw), F32),
            jax.ShapeDtypeStruct((n, aw), F32),
            jax.ShapeDtypeStruct((n, aw), BF16),
            jax.ShapeDtypeStruct((batch, aw, t), BF16),
            jax.ShapeDtypeStruct((n // tm, tm // MOBA_BLOCK, aw), F32),
            jax.ShapeDtypeStruct((n, SGU_WIDTH), BF16),
        ]
        out_specs = [
            pl.BlockSpec((None, aw, tm), col),
            pl.BlockSpec((tm, aw), row),
            pl.BlockSpec((tm, aw), row),
            pl.BlockSpec((tm, aw), row),
            pl.BlockSpec((None, aw, tm), col),
            pl.BlockSpec((None, tm // MOBA_BLOCK, aw), lambda i: (i, 0, 0)),
            pl.BlockSpec((tm, SGU_WIDTH), row),
        ]
    else:
        out_shape = [
            jax.ShapeDtypeStruct((n, aw), F32),
            jax.ShapeDtypeStruct((n, aw), F32),
            jax.ShapeDtypeStruct((n, aw), F32),
            jax.ShapeDtypeStruct((n, SGU_WIDTH), BF16),
            jax.ShapeDtypeStruct((n, SGU_WIDTH), F32),
        ]
        out_specs = [pl.BlockSpec((tm, aw), row)] * 3 + [pl.BlockSpec((tm, SGU_WIDTH), row)] * 2
    return pl.pallas_call(
        functools.partial(_inproj_kernel, prompt=prompt, chunk_len=chunk_len),
        out_shape=out_shape,
        grid=(n // tm,),
        in_specs=in_specs,
        out_specs=out_specs,
        compiler_params=_params("parallel"),
        name="inproj_prompt" if prompt else "inproj_sample",
    )(x, g, w_in, ws_tiled, bias_rows, g_sgu)


def _moba_prompt_kernel(qt_ref, kb_ref, vt_ref, km_ref, o_ref, base_ref, bias_ref, qmb_ref, sc_ref, acc_ref):
    blk = MOBA_BLOCK
    i = pl.program_id(1)
    n_blocks = km_ref.shape[0]
    pair_w = 2 * HEAD_DIM
    pair_row = lax.broadcasted_iota(jnp.int32, (pair_w, blk), 0)
    blk_idx = lax.broadcasted_iota(jnp.int32, (n_blocks, blk), 0)
    pairs = [slice((h // 2) * pair_w, (h // 2 + 1) * pair_w) for h in range(ATT_HEADS)]
    heads = [slice(h * HEAD_DIM, (h + 1) * HEAD_DIM) for h in range(ATT_HEADS)]
    slopes2 = [s * _LOG2E for s in _ALIBI_SLOPES]

    @pl.when(i == 0)
    def _():
        rk = lax.broadcasted_iota(jnp.int32, (blk, blk), 0)
        rq = lax.broadcasted_iota(jnp.int32, (blk, blk), 1)
        dist = (rq - rk).astype(F32)
        for h in range(ATT_HEADS):
            base_ref[0, h] = -slopes2[h] * dist
            base_ref[1, h] = jnp.where(rq >= rk, -slopes2[h] * dist, NEG)

    for h in range(ATT_HEADS):
        qm = jnp.where(pair_row // HEAD_DIM == h % 2, qt_ref[pairs[h], :], 0.0)
        qmb_ref[h] = (qm * _LOG2E).astype(BF16)
        gate = jnp.dot(km_ref[:, pairs[h]], qm, precision=_HIGHEST, preferred_element_type=F32)
        sel = _top_mask(gate, 0, i, blk_idx)
        bias_ref[h * n_blocks:(h + 1) * n_blocks, :] = jnp.where(
            sel | (blk_idx == i), (-slopes2[h] * blk) * (i - blk_idx).astype(F32), NEG)
    acc_ref[...] = jnp.zeros(acc_ref.shape, F32)

    def score_body(j, maxima):
        start = pl.multiple_of(j * blk, blk)
        is_own = (j == i).astype(jnp.int32)
        out = []
        for h in range(ATT_HEADS):
            s = (_dot(kb_ref[pl.ds(start, blk), pairs[h]], qmb_ref[h]) + base_ref[is_own, h]
                 + bias_ref[pl.ds(h * n_blocks + j, 1), :])
            sc_ref[h, j] = s
            out.append(jnp.maximum(maxima[h], jnp.max(s, axis=0, keepdims=True)))
        return tuple(out)

    maxima = lax.fori_loop(0, i + 1, score_body, tuple(jnp.full((1, blk), NEG, F32) for _ in range(ATT_HEADS)))

    def value_body(j, sums):
        start = pl.multiple_of(j * blk, blk)
        out = []
        for h in range(ATT_HEADS):
            p = jnp.exp2(sc_ref[h, j] - maxima[h])
            out.append(sums[h] + jnp.sum(p, axis=0, keepdims=True))
            acc_ref[heads[h], :] += _dot(vt_ref[heads[h], pl.ds(start, blk)], p.astype(BF16))
        return tuple(out)

    sums = lax.fori_loop(0, i + 1, value_body, tuple(jnp.zeros((1, blk), F32) for _ in range(ATT_HEADS)))
    outs = [acc_ref[heads[h], :] / sums[h] for h in range(ATT_HEADS)]
    o_ref[...] = jnp.concatenate(outs, axis=0).T.astype(BF16)


def _moba_prompt(qt, kb, vt, kmean):
    batch, aw, t = qt.shape
    n_blocks = t // MOBA_BLOCK
    assert t % MOBA_BLOCK == 0 and kmean.shape == (batch, n_blocks, aw)
    return pl.pallas_call(
        _moba_prompt_kernel,
        out_shape=jax.ShapeDtypeStruct((batch, t, aw), BF16),
        grid=(batch, n_blocks),
        in_specs=[
            pl.BlockSpec((None, aw, MOBA_BLOCK), lambda b, i: (b, 0, i)),
            _resident((None, t, aw), lambda b, i: (b, 0, 0)),
            _resident((None, aw, t), lambda b, i: (b, 0, 0)),
            _resident((None, n_blocks, aw), lambda b, i: (b, 0, 0)),
        ],
        out_specs=pl.BlockSpec((None, MOBA_BLOCK, aw), lambda b, i: (b, i, 0)),
        scratch_shapes=[
            pltpu.VMEM((2, ATT_HEADS, MOBA_BLOCK, MOBA_BLOCK), F32),
            pltpu.VMEM((ATT_HEADS * n_blocks, MOBA_BLOCK), F32),
            pltpu.VMEM((ATT_HEADS, 2 * HEAD_DIM, MOBA_BLOCK), BF16),
            pltpu.VMEM((ATT_HEADS, n_blocks, MOBA_BLOCK, MOBA_BLOCK), F32),
            pltpu.VMEM((ATT_WIDTH, MOBA_BLOCK), F32),
        ],
        compiler_params=_params("parallel", "arbitrary"),
        name="moba_prompt",
    )(qt, kb, vt, kmean)


_ROWS_PER_QUERY = 8


def _block_diag_queries(q, head_dim):
    s_len, w = q.shape
    lane_head = lax.broadcasted_iota(jnp.int32, (_ROWS_PER_QUERY, w), 1) // head_dim
    row_head = lax.broadcasted_iota(jnp.int32, (_ROWS_PER_QUERY, w), 0)
    keep = lane_head == row_head
    parts = [jnp.where(keep, jnp.broadcast_to(q[s:s + 1, :], (_ROWS_PER_QUERY, w)), 0.0) for s in range(s_len)]
    return jnp.concatenate(parts, axis=0), keep


def _pick_block_diag(r, keep, s_len):
    w = r.shape[1]
    keep_all = jnp.concatenate([keep] * s_len, axis=0)
    return jnp.sum(jnp.where(keep_all, r, 0.0).reshape(s_len, _ROWS_PER_QUERY, w), axis=1)


def _moba_sample_kernel(pt_ref, q_ref, ko_ref, vo_ref, *rest, n_pages, past_len):
    del pt_ref
    k_pages, v_pages, o_ref = rest[:n_pages], rest[n_pages:2 * n_pages], rest[2 * n_pages]
    s_len, n_heads, hd = q_ref.shape
    blk = MOBA_BLOCK
    pages_per_block = blk // PAGE_SIZE
    n_blocks = past_len // blk
    n_rows = s_len * n_heads
    page_rows = PAGE_SIZE * n_heads
    rep = lambda a: jnp.concatenate([a] * s_len, axis=0)

    q = q_ref[...].reshape(n_rows, hd)
    qb = q.astype(BF16)
    row = lax.broadcasted_iota(jnp.int32, (n_rows, 1), 0)
    row_head = row % n_heads
    slope = jnp.zeros((n_rows, 1), F32)
    for h in range(n_heads):
        slope = jnp.where(row_head == h, _ALIBI_SLOPES[h], slope)
    qpos = past_len + row // n_heads

    scores, means = [], []
    for n in range(n_blocks):
        ksum = jnp.zeros((n_heads, hd), F32)
        for t in range(pages_per_block):
            kp = k_pages[n * pages_per_block + t][...]
            ksum = ksum + jnp.sum(kp, axis=0)
            scores.append(_dot_nt(qb, kp.reshape(page_rows, hd).astype(BF16)))
        means.append(ksum * (1.0 / blk))
    blk_lane = lax.broadcasted_iota(jnp.int32, (n_rows, n_blocks), 1)
    gate = jnp.zeros((n_rows, n_blocks), F32)
    for n in range(n_blocks):
        gate = jnp.where(blk_lane == n, jnp.sum(q * rep(means[n]), axis=1, keepdims=True), gate)
    sel = _top_mask(gate, 1, n_blocks, blk_lane)
    unselected = jnp.where(sel, 0.0, NEG)

    key_lane = lax.broadcasted_iota(jnp.int32, (n_rows, page_rows), 1)
    in_page = jnp.where(key_lane % n_heads == row_head,
                        -slope * (qpos - key_lane // n_heads).astype(F32), NEG)
    for pg in range(n_pages):
        page_term = slope * float(pg * PAGE_SIZE) + unselected[:, pg // pages_per_block:pg // pages_per_block + 1]
        scores[pg] = scores[pg] + in_page + page_term
    own = []
    for t in range(s_len):
        s_t = jnp.sum(q * rep(ko_ref[t]), axis=1, keepdims=True)
        d_t = qpos - (past_len + t)
        own.append(jnp.where(d_t >= 0, s_t - slope * d_t.astype(F32), NEG))

    m_all = scores[0]
    for s_pg in scores[1:]:
        m_all = jnp.maximum(m_all, s_pg)
    m = jnp.max(m_all, axis=1, keepdims=True)
    for s_t in own:
        m = jnp.maximum(m, s_t)
    l = jnp.zeros((n_rows, 1), F32)
    acc = jnp.zeros((n_rows, hd), F32)
    for t in range(s_len):
        p_t = jnp.exp(own[t] - m)
        l = l + p_t
        acc = acc + p_t * rep(vo_ref[t])
    p_sum = jnp.zeros((n_rows, page_rows), F32)
    for pg in range(n_pages):
        p_pg = jnp.exp(scores[pg] - m)
        p_sum = p_sum + p_pg
        acc = acc + _dot(p_pg.astype(BF16), v_pages[pg][...].reshape(page_rows, hd).astype(BF16))
    l = l + jnp.sum(p_sum, axis=1, keepdims=True)
    o_ref[...] = (acc / l).reshape(s_len, n_heads, hd)


def _moba_sample(page_table, q4, k4, v4, cache_k, cache_v, layer):
    n_req, s_len, n_heads, hd = q4.shape
    n_pages = page_table.shape[1]
    past_len = n_pages * PAGE_SIZE
    assert past_len % MOBA_BLOCK == 0 and s_len <= MOBA_BLOCK and MOBA_BLOCK % PAGE_SIZE == 0
    assert past_len // MOBA_BLOCK >= MOBA_TOPK
    new = pl.BlockSpec((None, s_len, n_heads, hd), lambda b, pt: (b, 0, 0, 0))
    page_specs = [
        pl.BlockSpec((None, None, PAGE_SIZE, n_heads, hd), lambda b, pt, p=p: (layer, pt[b, p], 0, 0, 0))
        for p in range(n_pages)
    ]
    return pl.pallas_call(
        functools.partial(_moba_sample_kernel, n_pages=n_pages, past_len=past_len),
        out_shape=jax.ShapeDtypeStruct((n_req, s_len, n_heads, hd), F32),
        grid_spec=pltpu.PrefetchScalarGridSpec(
            num_scalar_prefetch=1,
            grid=(n_req,),
            in_specs=[new, new, new] + page_specs + page_specs,
            out_specs=new,
        ),
        compiler_params=_params("parallel"),
        name="moba_sample",
    )(page_table, q4, k4, v4, *([cache_k] * n_pages), *([cache_v] * n_pages))


def _mix_out(x_ref, oatt_ref, osgu_ref, wo_ref):
    return (x_ref[...] + _dot(oatt_ref[...].astype(BF16), wo_ref[0:ATT_WIDTH, :])
            + _dot(osgu_ref[...], wo_ref[ATT_WIDTH:, :]))


def _outx_prompt_kernel(x_ref, oatt_ref, osgu_ref, wo_ref, gx_ref, wxq_ref, mk_ref, mv_ref, wxo_ref, o_ref):
    tm = x_ref.shape[0]
    x2 = _mix_out(x_ref, oatt_ref, osgu_ref, wo_ref)
    qx = _dot(_rms(x2, gx_ref[...]).astype(BF16), wxq_ref[...]) * (X_HEAD_DIM ** -0.5)
    mk = mk_ref[...].astype(BF16)
    mv = mv_ref[...].astype(BF16)
    pair_w = 2 * X_HEAD_DIM
    lane = lax.broadcasted_iota(jnp.int32, (tm, pair_w), 1)
    pieces = []
    for hp in range(X_HEADS // 2):
        pair = slice(hp * pair_w, (hp + 1) * pair_w)
        q_pair = qx[:, pair]
        res = []
        for sub in range(2):
            qm = jnp.where(lane // X_HEAD_DIM == sub, q_pair, 0.0).astype(BF16)
            s = _dot_nt(qm, mk[:, pair])
            p = jnp.exp(s - jnp.max(s, axis=1, keepdims=True))
            p = p / jnp.sum(p, axis=1, keepdims=True)
            res.append(_dot(p.astype(BF16), mv[:, pair]))
        pieces.append(jnp.where(lane // X_HEAD_DIM == 0, res[0], res[1]))
    ox = jnp.concatenate(pieces, axis=1).astype(BF16)
    o_ref[...] = x2 + _dot(ox, wxo_ref[...])


def _outx_prompt(x, oatt, osgu, layer, w_out, g_x, w_xq, mem_k, mem_v, w_xo, *, batch):
    n, d = x.shape
    tm = ROW_TILE
    tiles_per_seq = n // batch // tm
    n_mem = mem_k.shape[2]
    row = lambda i: (i, 0)
    mem = pl.BlockSpec((None, None, n_mem, X_WIDTH), lambda i: (layer, i // tiles_per_seq, 0, 0))
    return pl.pallas_call(
        _outx_prompt_kernel,
        out_shape=jax.ShapeDtypeStruct((n, d), F32),
        grid=(n // tm,),
        in_specs=[
            pl.BlockSpec((tm, d), row),
            pl.BlockSpec((tm, ATT_WIDTH), row),
            pl.BlockSpec((tm, SGU_WIDTH), row),
            _resident((None,) + w_out.shape[1:], lambda i: (layer, 0, 0)),
            _resident((None, 1, d), lambda i: (layer, 0, 0)),
            _resident((None, d, X_WIDTH), lambda i: (layer, 0, 0)),
            mem,
            mem,
            _resident((None, X_WIDTH, d), lambda i: (layer, 0, 0)),
        ],
        out_specs=pl.BlockSpec((tm, d), row),
        compiler_params=_params("parallel"),
        name="outx_prompt",
    )(x, oatt, osgu, w_out, g_x, w_xq, mem_k, mem_v, w_xo)


def _outq_sample_kernel(x_ref, oatt_ref, osgu_ref, wo_ref, gx_ref, wxq_ref, x2_ref, q_ref):
    x2 = _mix_out(x_ref, oatt_ref, osgu_ref, wo_ref)
    x2_ref[...] = x2
    q_ref[...] = _dot(_rms(x2, gx_ref[...]).astype(BF16), wxq_ref[...]) * (X_HEAD_DIM ** -0.5)


def _outq_sample(x, oatt, osgu, layer, w_out, g_x, w_xq):
    n, d = x.shape
    tm = ROW_TILE
    row = lambda i: (i, 0)
    return pl.pallas_call(
        _outq_sample_kernel,
        out_shape=[jax.ShapeDtypeStruct((n, d), F32), jax.ShapeDtypeStruct((n, X_WIDTH), F32)],
        grid=(n // tm,),
        in_specs=[
            pl.BlockSpec((tm, d), row),
            pl.BlockSpec((tm, ATT_WIDTH), row),
            pl.BlockSpec((tm, SGU_WIDTH), row),
            _resident((None,) + w_out.shape[1:], lambda i: (layer, 0, 0)),
            _resident((None, 1, d), lambda i: (layer, 0, 0)),
            _resident((None, d, X_WIDTH), lambda i: (layer, 0, 0)),
        ],
        out_specs=[pl.BlockSpec((tm, d), row), pl.BlockSpec((tm, X_WIDTH), row)],
        compiler_params=_params("parallel"),
        name="outq_sample",
    )(x, oatt, osgu, w_out, g_x, w_xq)


_XATT_REQ_TILE = 8


def _xatt_sample_kernel(q_ref, mk_ref, mv_ref, o_ref):
    s_len = q_ref.shape[1]
    for b in range(q_ref.shape[0]):
        qbd, keep = _block_diag_queries(q_ref[b], X_HEAD_DIM)
        s = _dot_nt(qbd.astype(BF16), mk_ref[b].astype(BF16))
        p = jnp.exp(s - jnp.max(s, axis=1, keepdims=True))
        p = p / jnp.sum(p, axis=1, keepdims=True)
        r = _dot(p.astype(BF16), mv_ref[b].astype(BF16))
        o_ref[b] = _pick_block_diag(r, keep, s_len)


def _xatt_sample(q3, mem_k, mem_v, layer):
    n_req, s_len, _ = q3.shape
    n_mem = mem_k.shape[2]
    bt = _XATT_REQ_TILE
    assert n_req % bt == 0
    mem = pl.BlockSpec((None, bt, n_mem, X_WIDTH), lambda i: (layer, i, 0, 0))
    blk = pl.BlockSpec((bt, s_len, X_WIDTH), lambda i: (i, 0, 0))
    return pl.pallas_call(
        _xatt_sample_kernel,
        out_shape=jax.ShapeDtypeStruct((n_req, s_len, X_WIDTH), F32),
        grid=(n_req // bt,),
        in_specs=[blk, mem, mem],
        out_specs=blk,
        compiler_params=_params("parallel"),
        name="xatt_sample",
    )(q3, mem_k, mem_v)


def _memkv_kernel(mem_ref, g_ref, wk_ref, wv_ref, mk_ref, mv_ref):
    hm = _rms(mem_ref[...], g_ref[...]).astype(BF16)
    mk_ref[...] = _dot(hm, wk_ref[...])
    mv_ref[...] = _dot(hm, wv_ref[...])


def _memkv(mem, g_mem, w_xk, w_xv):
    n, d = mem.shape
    depth = g_mem.shape[0]
    tm = min(ROW_TILE, n)
    assert n % tm == 0
    w = pl.BlockSpec((None, d, X_WIDTH), lambda l, i: (l, 0, 0))
    out = pl.BlockSpec((None, tm, X_WIDTH), lambda l, i: (l, i, 0))
    return pl.pallas_call(
        _memkv_kernel,
        out_shape=[jax.ShapeDtypeStruct((depth, n, X_WIDTH), F32)] * 2,
        grid=(depth, n // tm),
        in_specs=[pl.BlockSpec((tm, d), lambda l, i: (i, 0)), pl.BlockSpec((None, 1, d), lambda l, i: (l, 0, 0)), w, w],
        out_specs=[out, out],
        compiler_params=_params("parallel", "parallel"),
        name="memkv",
    )(mem, g_mem, w_xk, w_xv)


def kernel(x_prompt, x_sample, cache_k, cache_v, cache_mem_k, cache_mem_v, page_table, mem_prompt, g_ffa, w_ffa_in, w_ffa_out, g_mix, w_in, w_out, w_sgu_s, b_sgu_s, g_sgu, g_x, w_xq, g_mem, w_xk, w_xv, w_xo, g_ffb, w_ffb_in, w_ffb_out, g_final):
    bp, t, d = x_prompt.shape
    bs, s_len, _ = x_sample.shape
    depth = g_ffa.shape[0]
    n_mem = mem_prompt.shape[1]
    assert t % SGU_CHUNK == 0 and SGU_CHUNK % s_len == 0 and s_len < SGU_CHUNK

    bf = lambda w: w.astype(BF16)
    w_ffa_in, w_ffa_out, w_ffb_in, w_ffb_out = bf(w_ffa_in), bf(w_ffa_out), bf(w_ffb_in), bf(w_ffb_out)
    w_in, w_out, w_xq, w_xk, w_xv, w_xo = bf(w_in), bf(w_out), bf(w_xq), bf(w_xk), bf(w_xv), bf(w_xo)
    g3 = lambda g: g.reshape(depth, 1, g.shape[-1])
    g_ffa, g_mix, g_sgu, g_x, g_mem, g_ffb = g3(g_ffa), g3(g_mix), g3(g_sgu), g3(g_x), g3(g_mem), g3(g_ffb)
    g_final = g_final.reshape(1, d)

    reps = SGU_CHUNK // s_len
    ws_prompt = w_sgu_s
    ws_sample = jnp.tile(w_sgu_s[:, :, :s_len, :s_len], (1, 1, reps, reps))
    bias_prompt = jnp.repeat(jnp.swapaxes(b_sgu_s, 1, 2), SGU_GROUP_DIM, axis=2)
    bias_sample = jnp.tile(bias_prompt[:, :s_len], (1, reps, 1))

    cache_mem_k = cache_mem_k.reshape(depth, bs, n_mem, X_WIDTH)
    cache_mem_v = cache_mem_v.reshape(depth, bs, n_mem, X_WIDTH)

    mk_p, mv_p = _memkv(mem_prompt.reshape(bp * n_mem, d), g_mem, w_xk, w_xv)
    mk_p4 = mk_p.reshape(depth, bp, n_mem, X_WIDTH)
    mv_p4 = mv_p.reshape(depth, bp, n_mem, X_WIDTH)

    xp = x_prompt.reshape(bp * t, d)
    xs = x_sample.reshape(bs * s_len, d)
    kp_l, vp_l, ks_l, vs_l, vrow_l = [], [], [], [], []
    for l in range(depth):
        last = l == depth - 1
        xp = _ffn(xp, l, g_ffa, w_ffa_in, w_ffa_out)
        qt, k, v, kb, vt, kmean, osgu = _inproj(
            xp, l, g_mix, w_in, ws_prompt, bias_prompt, g_sgu, prompt=True, chunk_len=SGU_CHUNK, batch=bp)
        kp_l.append(k)
        vp_l.append(v)
        oatt = _moba_prompt(qt, kb.reshape(bp, t, ATT_WIDTH), vt, kmean.reshape(bp, t // MOBA_BLOCK, ATT_WIDTH))
        xp = _outx_prompt(xp, oatt.reshape(bp * t, ATT_WIDTH), osgu, l, w_out, g_x, w_xq, mk_p4, mv_p4, w_xo, batch=bp)
        xp = _ffn(xp, l, g_ffb, w_ffb_in, w_ffb_out, final_g=g_final if last else None)
        xs = _ffn(xs, l, g_ffa, w_ffa_in, w_ffa_out)
        q, k, v, osgu, vn = _inproj(
            xs, l, g_mix, w_in, ws_sample, bias_sample, g_sgu, prompt=False, chunk_len=s_len, batch=bs)
        to3 = lambda a: a.reshape(bs, s_len, a.shape[-1])
        to_heads = lambda a: a.reshape(bs, s_len, ATT_HEADS, HEAD_DIM)
        k, v = to_heads(k), to_heads(v)
        ks_l.append(k)
        vs_l.append(v)
        vrow_l.append(vn)
        oatt = _moba_sample(page_table, to_heads(q), k, v, cache_k, cache_v, l)
        x2, qx = _outq_sample(xs, oatt.reshape(bs * s_len, ATT_WIDTH), osgu, l, w_out, g_x, w_xq)
        ox = _xatt_sample(to3(qx), cache_mem_k, cache_mem_v, l)
        xs = _ffn(x2, l, g_ffb, w_ffb_in, w_ffb_out, pre=(ox.reshape(bs * s_len, X_WIDTH), w_xo),
                  final_g=g_final if last else None)

    y_prompt = xp.reshape(bp, t, d)
    y_sample = xs.reshape(bs, s_len, d)
    k_prompt = jnp.stack(kp_l).reshape(depth, bp, t // PAGE_SIZE, PAGE_SIZE, ATT_HEADS, HEAD_DIM)
    v_prompt = jnp.stack(vp_l).reshape(depth, bp, t // PAGE_SIZE, PAGE_SIZE, ATT_HEADS, HEAD_DIM)
    mem_k_prompt = mk_p.reshape(depth, bp, n_mem, X_HEADS, X_HEAD_DIM)
    mem_v_prompt = mv_p.reshape(depth, bp, n_mem, X_HEADS, X_HEAD_DIM)
    k_sample = jnp.stack(ks_l).reshape(depth, bs, s_len, ATT_HEADS, HEAD_DIM)
    v_sample = jnp.stack(vs_l).reshape(depth, bs, s_len, ATT_HEADS, HEAD_DIM)
    sgu_v_sample = jnp.stack(vrow_l).reshape(depth, bs, s_len, SGU_WIDTH)
    return (y_prompt, y_sample, k_prompt, v_prompt, mem_k_prompt, mem_v_prompt, k_sample, v_sample, sgu_v_sample)
```

```python
import functools

import numpy as np
import jax
import jax.numpy as jnp
from jax import lax
from jax.experimental import pallas as pl
from jax.experimental.pallas import tpu as pltpu

F32 = jnp.float32
BF16 = jnp.bfloat16

EPS = 1e-6
NEG = -1e30

ATT_HEADS = 8
HEAD_DIM = 64
ATT_WIDTH = ATT_HEADS * HEAD_DIM
MOBA_BLOCK = 256
MOBA_TOPK = 3
SGU_GROUPS = 4
SGU_GROUP_DIM = 128
SGU_CHUNK = 128
SGU_WIDTH = SGU_GROUPS * SGU_GROUP_DIM
X_HEADS = 4
X_HEAD_DIM = 64
X_WIDTH = X_HEADS * X_HEAD_DIM
PAGE_SIZE = 128

V7X_LANES = 128
V7X_VMEM_BYTES = 64 * 1024 * 1024
VMEM_LIMIT_BYTES = V7X_VMEM_BYTES - 8 * 1024 * 1024

ROW_TILE = 512
FF_CHUNK = 256

_ALIBI_SLOPES = tuple(float(2.0 ** (-8.0 * (h + 1) / ATT_HEADS)) for h in range(ATT_HEADS))
_HIGHEST = lax.Precision.HIGHEST
_LOG2E = float(np.log2(np.e))


def _params(*semantics):
    return pltpu.CompilerParams(dimension_semantics=semantics, vmem_limit_bytes=VMEM_LIMIT_BYTES)


def _resident(block, index_map):
    return pl.BlockSpec(block, index_map, pipeline_mode=pl.Buffered(1))


def _rms(x, g):
    return x * lax.rsqrt(jnp.mean(x * x, axis=-1, keepdims=True) + EPS) * g


def _dot(a, b):
    return jnp.dot(a, b, preferred_element_type=F32)


def _dot_nt(a, b, precision=None):
    return lax.dot_general(a, b, (((1,), (1,)), ((), ())), precision=precision, preferred_element_type=F32)


def _top_mask(gate, axis, n_valid, idx):
    n = gate.shape[axis]
    rank = jnp.zeros(gate.shape, jnp.int32)
    for other in range(n):
        g_o = lax.slice_in_dim(gate, other, other + 1, axis=axis)
        beats = (g_o > gate) | ((g_o == gate) & (other < idx))
        rank = rank + jnp.where(beats, jnp.where(other < n_valid, 1, 0), 0)
    return (rank < MOBA_TOPK) & (idx < n_valid)


def _ffn_kernel(*refs, pre_add, final_norm):
    it = iter(refs)
    x_ref = next(it)
    if pre_add:
        ox_ref, wxo_ref = next(it), next(it)
    g_ref, wa_ref, wb_ref, wo_ref = next(it), next(it), next(it), next(it)
    if final_norm:
        gf_ref = next(it)
    o_ref, s_ref = next(it), next(it)

    x = x_ref[...]
    if pre_add:
        x = x + _dot(ox_ref[...].astype(BF16), wxo_ref[...])
    h = _rms(x, g_ref[...]).astype(BF16)
    d_ff = wa_ref.shape[1]
    for c in range(d_ff // FF_CHUNK):
        cols = slice(c * FF_CHUNK, (c + 1) * FF_CHUNK)
        a = _dot(h, wa_ref[:, cols])
        b = _dot(h, wb_ref[:, cols])
        s_ref[:, cols] = (jax.nn.silu(a) * b).astype(BF16)
    y = x + 0.5 * _dot(s_ref[...], wo_ref[...])
    if final_norm:
        y = _rms(y, gf_ref[...])
    o_ref[...] = y


def _ffn(x, layer, g, w_in, w_out, *, pre=None, final_g=None):
    n, d = x.shape
    d_ff = w_out.shape[1]
    assert d_ff % FF_CHUNK == 0 and n % ROW_TILE == 0
    tm = ROW_TILE
    row = lambda i: (i, 0)
    args, specs = [x], [pl.BlockSpec((tm, d), row)]
    if pre is not None:
        ox, w_xo = pre
        args += [ox, w_xo]
        specs += [pl.BlockSpec((tm, ox.shape[1]), row), _resident((None,) + w_xo.shape[1:], lambda i: (layer, 0, 0))]
    args += [g, w_in, w_in, w_out]
    specs += [
        _resident((None, 1, d), lambda i: (layer, 0, 0)),
        _resident((None, d, d_ff), lambda i: (layer, 0, 0)),
        _resident((None, d, d_ff), lambda i: (layer, 0, 1)),
        _resident((None, d_ff, d), lambda i: (layer, 0, 0)),
    ]
    if final_g is not None:
        args.append(final_g)
        specs.append(_resident((1, d), lambda i: (0, 0)))
    return pl.pallas_call(
        functools.partial(_ffn_kernel, pre_add=pre is not None, final_norm=final_g is not None),
        out_shape=jax.ShapeDtypeStruct((n, d), F32),
        grid=(n // tm,),
        in_specs=specs,
        out_specs=pl.BlockSpec((tm, d), row),
        scratch_shapes=[pltpu.VMEM((tm, d_ff), BF16)],
        compiler_params=_params("parallel"),
        name="ffn",
    )(*args)


def _inproj_kernel(x_ref, g_ref, w_ref, ws_ref, bias_ref, gs_ref, *outs, prompt, chunk_len):
    tm = x_ref.shape[0]
    h = _rms(x_ref[...], g_ref[...]).astype(BF16)
    aw = ATT_WIDTH
    zq = _dot(h, w_ref[:, 0:aw]) * (HEAD_DIM ** -0.5)
    zk = _dot(h, w_ref[:, aw:2 * aw])
    zv = _dot(h, w_ref[:, 2 * aw:3 * aw])
    u = jax.nn.gelu(_dot(h, w_ref[:, 3 * aw:3 * aw + SGU_WIDTH]))
    vg = jax.nn.gelu(_dot(h, w_ref[:, 3 * aw + SGU_WIDTH:]))

    if prompt:
        qt_ref, k_ref, v_ref, kb_ref, vt_ref, km_ref, osgu_ref = outs
        qt_ref[...] = zq.T
        kb_ref[...] = zk.astype(BF16)
        vt_ref[...] = zv.T.astype(BF16)
        km_ref[...] = jnp.mean(zk.reshape(tm // MOBA_BLOCK, MOBA_BLOCK, aw), axis=1)
    else:
        q_ref, k_ref, v_ref, osgu_ref, vn_ref = outs
        q_ref[...] = zq
    k_ref[...] = zk
    v_ref[...] = zv

    r = lax.broadcasted_iota(jnp.int32, (SGU_CHUNK, SGU_CHUNK), 0)
    c = lax.broadcasted_iota(jnp.int32, (SGU_CHUNK, SGU_CHUNK), 1)
    mix_mask = (r // chunk_len == c // chunk_len) & (c <= r)
    for grp in range(SGU_GROUPS):
        cols = slice(grp * SGU_GROUP_DIM, (grp + 1) * SGU_GROUP_DIM)
        vgg = vg[:, cols]
        vn = vgg * lax.rsqrt(jnp.mean(vgg * vgg, axis=-1, keepdims=True) + EPS) * gs_ref[:, cols]
        if not prompt:
            vn_ref[:, cols] = vn
        vnb = vn.astype(BF16)
        wsm = jnp.where(mix_mask, ws_ref[grp], 0.0).astype(BF16)
        for ch in range(tm // SGU_CHUNK):
            rows = slice(ch * SGU_CHUNK, (ch + 1) * SGU_CHUNK)
            mixed = _dot(wsm, vnb[rows]) + bias_ref[:, cols]
            osgu_ref[rows, cols] = (u[rows, cols] * mixed).astype(BF16)


def _inproj(x, layer, g, w_in, ws_tiled, bias_rows, g_sgu, *, prompt, chunk_len, batch):
    n, d = x.shape
    tm = ROW_TILE
    assert n % tm == 0 and tm % MOBA_BLOCK == 0 and tm % SGU_CHUNK == 0
    row = lambda i: (i, 0)
    aw = ATT_WIDTH
    in_specs = [
        pl.BlockSpec((tm, d), row),
        _resident((None, 1, d), lambda i: (layer, 0, 0)),
        _resident((None, d, w_in.shape[2]), lambda i: (layer, 0, 0)),
        _resident((None, SGU_GROUPS, SGU_CHUNK, SGU_CHUNK), lambda i: (layer, 0, 0, 0)),
        _resident((None, SGU_CHUNK, SGU_WIDTH), lambda i: (layer, 0, 0)),
        _resident((None, 1, SGU_WIDTH), lambda i: (layer, 0, 0)),
    ]
    if prompt:
        t = n // batch
        tiles_per_seq = t // tm
        col = lambda i: (i // tiles_per_seq, 0, i % tiles_per_seq)
        out_shape = [
            jax.ShapeDtypeStruct((batch, aw, t), F32),
            jax.ShapeDtypeStruct((n, aw), F32),
            jax.ShapeDtypeStruct((n, aw), F32),
            jax.ShapeDtypeStruct((n, aw), BF16),
            jax.ShapeDtypeStruct((batch, aw, t), BF16),
            jax.ShapeDtypeStruct((n // tm, tm // MOBA_BLOCK, aw), F32),
            jax.ShapeDtypeStruct((n, SGU_WIDTH), BF16),
        ]
        out_specs = [
            pl.BlockSpec((None, aw, tm), col),
            pl.BlockSpec((tm, aw), row),
            pl.BlockSpec((tm, aw), row),
            pl.BlockSpec((tm, aw), row),
            pl.BlockSpec((None, aw, tm), col),
            pl.BlockSpec((None, tm // MOBA_BLOCK, aw), lambda i: (i, 0, 0)),
            pl.BlockSpec((tm, SGU_WIDTH), row),
        ]
    else:
        out_shape = [
            jax.ShapeDtypeStruct((n, aw), F32),
            jax.ShapeDtypeStruct((n, aw), F32),
            jax.ShapeDtypeStruct((n, aw), F32),
            jax.ShapeDtypeStruct((n, SGU_WIDTH), BF16),
            jax.ShapeDtypeStruct((n, SGU_WIDTH), F32),
        ]
        out_specs = [pl.BlockSpec((tm, aw), row)] * 3 + [pl.BlockSpec((tm, SGU_WIDTH), row)] * 2
    return pl.pallas_call(
        functools.partial(_inproj_kernel, prompt=prompt, chunk_len=chunk_len),
        out_shape=out_shape,
        grid=(n // tm,),
        in_specs=in_specs,
        out_specs=out_specs,
        compiler_params=_params("parallel"),
        name="inproj_prompt" if prompt else "inproj_sample",
    )(x, g, w_in, ws_tiled, bias_rows, g_sgu)


def _moba_prompt_kernel(qt_ref, kb_ref, vt_ref, km_ref, o_ref, base_ref, bias_ref, qmb_ref, sc_ref, acc_ref):
    blk = MOBA_BLOCK
    i = pl.program_id(1)
    n_blocks = km_ref.shape[0]
    pair_w = 2 * HEAD_DIM
    pair_row = lax.broadcasted_iota(jnp.int32, (pair_w, blk), 0)
    blk_idx = lax.broadcasted_iota(jnp.int32, (n_blocks, blk), 0)
    pairs = [slice((h // 2) * pair_w, (h // 2 + 1) * pair_w) for h in range(ATT_HEADS)]
    heads = [slice(h * HEAD_DIM, (h + 1) * HEAD_DIM) for h in range(ATT_HEADS)]
    slopes2 = [s * _LOG2E for s in _ALIBI_SLOPES]

    @pl.when(i == 0)
    def _():
        rk = lax.broadcasted_iota(jnp.int32, (blk, blk), 0)
        rq = lax.broadcasted_iota(jnp.int32, (blk, blk), 1)
        dist = (rq - rk).astype(F32)
        for h in range(ATT_HEADS):
            base_ref[0, h] = -slopes2[h] * dist
            base_ref[1, h] = jnp.where(rq >= rk, -slopes2[h] * dist, NEG)

    for h in range(ATT_HEADS):
        qm = jnp.where(pair_row // HEAD_DIM == h % 2, qt_ref[pairs[h], :], 0.0)
        qmb_ref[h] = (qm * _LOG2E).astype(BF16)
        gate = jnp.dot(km_ref[:, pairs[h]], qm, precision=_HIGHEST, preferred_element_type=F32)
        sel = _top_mask(gate, 0, i, blk_idx)
        bias_ref[h * n_blocks:(h + 1) * n_blocks, :] = jnp.where(
            sel | (blk_idx == i), (-slopes2[h] * blk) * (i - blk_idx).astype(F32), NEG)
    acc_ref[...] = jnp.zeros(acc_ref.shape, F32)

    def score_body(j, maxima):
        start = pl.multiple_of(j * blk, blk)
        is_own = (j == i).astype(jnp.int32)
        out = []
        for h in range(ATT_HEADS):
            s = (_dot(kb_ref[pl.ds(start, blk), pairs[h]], qmb_ref[h]) + base_ref[is_own, h]
                 + bias_ref[pl.ds(h * n_blocks + j, 1), :])
            sc_ref[h, j] = s
            out.append(jnp.maximum(maxima[h], jnp.max(s, axis=0, keepdims=True)))
        return tuple(out)

    maxima = lax.fori_loop(0, i + 1, score_body, tuple(jnp.full((1, blk), NEG, F32) for _ in range(ATT_HEADS)))

    def value_body(j, sums):
        start = pl.multiple_of(j * blk, blk)
        out = []
        for h in range(ATT_HEADS):
            p = jnp.exp2(sc_ref[h, j] - maxima[h])
            out.append(sums[h] + jnp.sum(p, axis=0, keepdims=True))
            acc_ref[heads[h], :] += _dot(vt_ref[heads[h], pl.ds(start, blk)], p.astype(BF16))
        return tuple(out)

    sums = lax.fori_loop(0, i + 1, value_body, tuple(jnp.zeros((1, blk), F32) for _ in range(ATT_HEADS)))
    outs = [acc_ref[heads[h], :] / sums[h] for h in range(ATT_HEADS)]
    o_ref[...] = jnp.concatenate(outs, axis=0).T.astype(BF16)


def _moba_prompt(qt, kb, vt, kmean):
    batch, aw, t = qt.shape
    n_blocks = t // MOBA_BLOCK
    assert t % MOBA_BLOCK == 0 and kmean.shape == (batch, n_blocks, aw)
    return pl.pallas_call(
        _moba_prompt_kernel,
        out_shape=jax.ShapeDtypeStruct((batch, t, aw), BF16),
        grid=(batch, n_blocks),
        in_specs=[
            pl.BlockSpec((None, aw, MOBA_BLOCK), lambda b, i: (b, 0, i)),
            _resident((None, t, aw), lambda b, i: (b, 0, 0)),
            _resident((None, aw, t), lambda b, i: (b, 0, 0)),
            _resident((None, n_blocks, aw), lambda b, i: (b, 0, 0)),
        ],
        out_specs=pl.BlockSpec((None, MOBA_BLOCK, aw), lambda b, i: (b, i, 0)),
        scratch_shapes=[
            pltpu.VMEM((2, ATT_HEADS, MOBA_BLOCK, MOBA_BLOCK), F32),
            pltpu.VMEM((ATT_HEADS * n_blocks, MOBA_BLOCK), F32),
            pltpu.VMEM((ATT_HEADS, 2 * HEAD_DIM, MOBA_BLOCK), BF16),
            pltpu.VMEM((ATT_HEADS, n_blocks, MOBA_BLOCK, MOBA_BLOCK), F32),
            pltpu.VMEM((ATT_WIDTH, MOBA_BLOCK), F32),
        ],
        compiler_params=_params("parallel", "arbitrary"),
        name="moba_prompt",
    )(qt, kb, vt, kmean)


_ROWS_PER_QUERY = 8


def _block_diag_queries(q, head_dim):
    s_len, w = q.shape
    lane_head = lax.broadcasted_iota(jnp.int32, (_ROWS_PER_QUERY, w), 1) // head_dim
    row_head = lax.broadcasted_iota(jnp.int32, (_ROWS_PER_QUERY, w), 0)
    keep = lane_head == row_head
    parts = [jnp.where(keep, jnp.broadcast_to(q[s:s + 1, :], (_ROWS_PER_QUERY, w)), 0.0) for s in range(s_len)]
    return jnp.concatenate(parts, axis=0), keep


def _pick_block_diag(r, keep, s_len):
    w = r.shape[1]
    keep_all = jnp.concatenate([keep] * s_len, axis=0)
    return jnp.sum(jnp.where(keep_all, r, 0.0).reshape(s_len, _ROWS_PER_QUERY, w), axis=1)


def _moba_sample_kernel(pt_ref, q_ref, ko_ref, vo_ref, *rest, n_pages, past_len):
    del pt_ref
    k_pages, v_pages, o_ref = rest[:n_pages], rest[n_pages:2 * n_pages], rest[2 * n_pages]
    s_len, aw = q_ref.shape
    blk = MOBA_BLOCK
    pages_per_block = blk // PAGE_SIZE
    n_blocks = past_len // blk
    n_rows = s_len * _ROWS_PER_QUERY

    q = q_ref[...]
    qbd, keep = _block_diag_queries(q, HEAD_DIM)
    qbd_b = qbd.astype(BF16)
    row = lax.broadcasted_iota(jnp.int32, (n_rows, 1), 0)
    row_head = row % _ROWS_PER_QUERY
    slope = jnp.zeros((n_rows, 1), F32)
    for h in range(ATT_HEADS):
        slope = jnp.where(row_head == h, _ALIBI_SLOPES[h], slope)
    qpos = past_len + row // _ROWS_PER_QUERY

    scores = []
    blk_lane = lax.broadcasted_iota(jnp.int32, (aw, n_blocks), 1)
    kmean_t = jnp.zeros((aw, n_blocks), F32)
    for n in range(n_blocks):
        ksum = jnp.zeros((aw, PAGE_SIZE), F32)
        for t in range(pages_per_block):
            kp = k_pages[n * pages_per_block + t][...]
            ksum = ksum + kp
            scores.append(_dot(qbd_b, kp.astype(BF16)))
        kmean_t = jnp.where(blk_lane == n, jnp.sum(ksum, axis=1, keepdims=True) * (1.0 / blk), kmean_t)
    q_t = jnp.concatenate([q, jnp.zeros((_ROWS_PER_QUERY - s_len, aw), F32)], axis=0).T
    gate = jnp.concatenate(
        [jnp.sum((q_t[:, s:s + 1] * kmean_t).reshape(ATT_HEADS, HEAD_DIM, n_blocks), axis=1) for s in range(s_len)],
        axis=0)
    sel = _top_mask(gate, 1, n_blocks, lax.broadcasted_iota(jnp.int32, (n_rows, n_blocks), 1))
    unselected = jnp.where(sel, 0.0, NEG)

    key_off = lax.broadcasted_iota(jnp.int32, (n_rows, PAGE_SIZE), 1)
    for pg in range(n_pages):
        n = pg // pages_per_block
        dist = (qpos - (pg * PAGE_SIZE + key_off)).astype(F32)
        scores[pg] = scores[pg] - slope * dist + unselected[:, n:n + 1]
    k_own, v_own = ko_ref[...], vo_ref[...]
    own = []
    for t in range(s_len):
        s_t = jnp.sum(qbd * k_own[t:t + 1, :], axis=1, keepdims=True)
        d_t = qpos - (past_len + t)
        own.append(jnp.where(d_t >= 0, s_t - slope * d_t.astype(F32), NEG))

    m_all = scores[0]
    for s_pg in scores[1:]:
        m_all = jnp.maximum(m_all, s_pg)
    m = jnp.max(m_all, axis=1, keepdims=True)
    for s_t in own:
        m = jnp.maximum(m, s_t)
    l = jnp.zeros((n_rows, 1), F32)
    acc = jnp.zeros((n_rows, aw), F32)
    for t in range(s_len):
        p_t = jnp.exp(own[t] - m)
        l = l + p_t
        acc = acc + p_t * v_own[t:t + 1, :]
    p_sum = jnp.zeros((n_rows, PAGE_SIZE), F32)
    for pg in range(n_pages):
        p_pg = jnp.exp(scores[pg] - m)
        p_sum = p_sum + p_pg
        acc = acc + _dot_nt(p_pg.astype(BF16), v_pages[pg][...].astype(BF16))
    l = l + jnp.sum(p_sum, axis=1, keepdims=True)
    o_ref[...] = _pick_block_diag(acc / l, keep, s_len)


def _moba_sample(page_table, q3, k3, v3, cache_kt, cache_vt, layer):
    n_req, s_len, aw = q3.shape
    n_pages = page_table.shape[1]
    past_len = n_pages * PAGE_SIZE
    assert past_len % MOBA_BLOCK == 0 and s_len <= _ROWS_PER_QUERY and MOBA_BLOCK % PAGE_SIZE == 0
    assert past_len // MOBA_BLOCK >= MOBA_TOPK and cache_kt.shape[2:] == (aw, PAGE_SIZE)
    new = pl.BlockSpec((None, s_len, aw), lambda b, pt: (b, 0, 0))
    page_specs = [
        pl.BlockSpec((None, None, aw, PAGE_SIZE), lambda b, pt, p=p: (layer, pt[b, p], 0, 0)) for p in range(n_pages)
    ]
    return pl.pallas_call(
        functools.partial(_moba_sample_kernel, n_pages=n_pages, past_len=past_len),
        out_shape=jax.ShapeDtypeStruct((n_req, s_len, aw), F32),
        grid_spec=pltpu.PrefetchScalarGridSpec(
            num_scalar_prefetch=1,
            grid=(n_req,),
            in_specs=[new, new, new] + page_specs + page_specs,
            out_specs=new,
        ),
        compiler_params=_params("parallel"),
        name="moba_sample",
    )(page_table, q3, k3, v3, *([cache_kt] * n_pages), *([cache_vt] * n_pages))


def _mix_out(x_ref, oatt_ref, osgu_ref, wo_ref):
    return (x_ref[...] + _dot(oatt_ref[...].astype(BF16), wo_ref[0:ATT_WIDTH, :])
            + _dot(osgu_ref[...], wo_ref[ATT_WIDTH:, :]))


def _outx_prompt_kernel(x_ref, oatt_ref, osgu_ref, wo_ref, gx_ref, wxq_ref, mk_ref, mv_ref, wxo_ref, o_ref):
    tm = x_ref.shape[0]
    x2 = _mix_out(x_ref, oatt_ref, osgu_ref, wo_ref)
    qx = _dot(_rms(x2, gx_ref[...]).astype(BF16), wxq_ref[...]) * (X_HEAD_DIM ** -0.5)
    mk = mk_ref[...].astype(BF16)
    mv = mv_ref[...].astype(BF16)
    pair_w = 2 * X_HEAD_DIM
    lane = lax.broadcasted_iota(jnp.int32, (tm, pair_w), 1)
    pieces = []
    for hp in range(X_HEADS // 2):
        pair = slice(hp * pair_w, (hp + 1) * pair_w)
        q_pair = qx[:, pair]
        res = []
        for sub in range(2):
            qm = jnp.where(lane // X_HEAD_DIM == sub, q_pair, 0.0).astype(BF16)
            s = _dot_nt(qm, mk[:, pair])
            p = jnp.exp(s - jnp.max(s, axis=1, keepdims=True))
            p = p / jnp.sum(p, axis=1, keepdims=True)
            res.append(_dot(p.astype(BF16), mv[:, pair]))
        pieces.append(jnp.where(lane // X_HEAD_DIM == 0, res[0], res[1]))
    ox = jnp.concatenate(pieces, axis=1).astype(BF16)
    o_ref[...] = x2 + _dot(ox, wxo_ref[...])


def _outx_prompt(x, oatt, osgu, layer, w_out, g_x, w_xq, mem_k, mem_v, w_xo, *, batch):
    n, d = x.shape
    tm = ROW_TILE
    tiles_per_seq = n // batch // tm
    n_mem = mem_k.shape[2]
    row = lambda i: (i, 0)
    mem = pl.BlockSpec((None, None, n_mem, X_WIDTH), lambda i: (layer, i // tiles_per_seq, 0, 0))
    return pl.pallas_call(
        _outx_prompt_kernel,
        out_shape=jax.ShapeDtypeStruct((n, d), F32),
        grid=(n // tm,),
        in_specs=[
            pl.BlockSpec((tm, d), row),
            pl.BlockSpec((tm, ATT_WIDTH), row),
            pl.BlockSpec((tm, SGU_WIDTH), row),
            _resident((None,) + w_out.shape[1:], lambda i: (layer, 0, 0)),
            _resident((None, 1, d), lambda i: (layer, 0, 0)),
            _resident((None, d, X_WIDTH), lambda i: (layer, 0, 0)),
            mem,
            mem,
            _resident((None, X_WIDTH, d), lambda i: (layer, 0, 0)),
        ],
        out_specs=pl.BlockSpec((tm, d), row),
        compiler_params=_params("parallel"),
        name="outx_prompt",
    )(x, oatt, osgu, w_out, g_x, w_xq, mem_k, mem_v, w_xo)


def _outq_sample_kernel(x_ref, oatt_ref, osgu_ref, wo_ref, gx_ref, wxq_ref, x2_ref, q_ref):
    x2 = _mix_out(x_ref, oatt_ref, osgu_ref, wo_ref)
    x2_ref[...] = x2
    q_ref[...] = _dot(_rms(x2, gx_ref[...]).astype(BF16), wxq_ref[...]) * (X_HEAD_DIM ** -0.5)


def _outq_sample(x, oatt, osgu, layer, w_out, g_x, w_xq):
    n, d = x.shape
    tm = ROW_TILE
    row = lambda i: (i, 0)
    return pl.pallas_call(
        _outq_sample_kernel,
        out_shape=[jax.ShapeDtypeStruct((n, d), F32), jax.ShapeDtypeStruct((n, X_WIDTH), F32)],
        grid=(n // tm,),
        in_specs=[
            pl.BlockSpec((tm, d), row),
            pl.BlockSpec((tm, ATT_WIDTH), row),
            pl.BlockSpec((tm, SGU_WIDTH), row),
            _resident((None,) + w_out.shape[1:], lambda i: (layer, 0, 0)),
            _resident((None, 1, d), lambda i: (layer, 0, 0)),
            _resident((None, d, X_WIDTH), lambda i: (layer, 0, 0)),
        ],
        out_specs=[pl.BlockSpec((tm, d), row), pl.BlockSpec((tm, X_WIDTH), row)],
        compiler_params=_params("parallel"),
        name="outq_sample",
    )(x, oatt, osgu, w_out, g_x, w_xq)


_XATT_REQ_TILE = 8


def _xatt_sample_kernel(q_ref, mkt_ref, mvt_ref, o_ref):
    s_len = q_ref.shape[1]
    for b in range(q_ref.shape[0]):
        qbd, keep = _block_diag_queries(q_ref[b], X_HEAD_DIM)
        s = _dot(qbd.astype(BF16), mkt_ref[b].astype(BF16))
        p = jnp.exp(s - jnp.max(s, axis=1, keepdims=True))
        p = p / jnp.sum(p, axis=1, keepdims=True)
        r = _dot_nt(p.astype(BF16), mvt_ref[b].astype(BF16))
        o_ref[b] = _pick_block_diag(r, keep, s_len)


def _xatt_sample(q3, mem_kt, mem_vt, layer):
    n_req, s_len, _ = q3.shape
    n_mem = mem_kt.shape[3]
    bt = _XATT_REQ_TILE
    assert n_req % bt == 0 and mem_kt.shape[2] == X_WIDTH
    mem = pl.BlockSpec((None, bt, X_WIDTH, n_mem), lambda i: (layer, i, 0, 0))
    blk = pl.BlockSpec((bt, s_len, X_WIDTH), lambda i: (i, 0, 0))
    return pl.pallas_call(
        _xatt_sample_kernel,
        out_shape=jax.ShapeDtypeStruct((n_req, s_len, X_WIDTH), F32),
        grid=(n_req // bt,),
        in_specs=[blk, mem, mem],
        out_specs=blk,
        compiler_params=_params("parallel"),
        name="xatt_sample",
    )(q3, mem_kt, mem_vt)


def _memkv_kernel(mem_ref, g_ref, wk_ref, wv_ref, mk_ref, mv_ref):
    hm = _rms(mem_ref[...], g_ref[...]).astype(BF16)
    mk_ref[...] = _dot(hm, wk_ref[...])
    mv_ref[...] = _dot(hm, wv_ref[...])


def _memkv(mem, g_mem, w_xk, w_xv):
    n, d = mem.shape
    depth = g_mem.shape[0]
    tm = min(ROW_TILE, n)
    assert n % tm == 0
    w = pl.BlockSpec((None, d, X_WIDTH), lambda l, i: (l, 0, 0))
    out = pl.BlockSpec((None, tm, X_WIDTH), lambda l, i: (l, i, 0))
    return pl.pallas_call(
        _memkv_kernel,
        out_shape=[jax.ShapeDtypeStruct((depth, n, X_WIDTH), F32)] * 2,
        grid=(depth, n // tm),
        in_specs=[pl.BlockSpec((tm, d), lambda l, i: (i, 0)), pl.BlockSpec((None, 1, d), lambda l, i: (l, 0, 0)), w, w],
        out_specs=[out, out],
        compiler_params=_params("parallel", "parallel"),
        name="memkv",
    )(mem, g_mem, w_xk, w_xv)


def kernel(x_prompt, x_sample, cache_k, cache_v, cache_mem_k, cache_mem_v, page_table, mem_prompt, g_ffa, w_ffa_in, w_ffa_out, g_mix, w_in, w_out, w_sgu_s, b_sgu_s, g_sgu, g_x, w_xq, g_mem, w_xk, w_xv, w_xo, g_ffb, w_ffb_in, w_ffb_out, g_final):
    bp, t, d = x_prompt.shape
    bs, s_len, _ = x_sample.shape
    depth = g_ffa.shape[0]
    n_mem = mem_prompt.shape[1]
    assert t % SGU_CHUNK == 0 and SGU_CHUNK % s_len == 0 and s_len < SGU_CHUNK

    bf = lambda w: w.astype(BF16)
    w_ffa_in, w_ffa_out, w_ffb_in, w_ffb_out = bf(w_ffa_in), bf(w_ffa_out), bf(w_ffb_in), bf(w_ffb_out)
    w_in, w_out, w_xq, w_xk, w_xv, w_xo = bf(w_in), bf(w_out), bf(w_xq), bf(w_xk), bf(w_xv), bf(w_xo)
    g3 = lambda g: g.reshape(depth, 1, g.shape[-1])
    g_ffa, g_mix, g_sgu, g_x, g_mem, g_ffb = g3(g_ffa), g3(g_mix), g3(g_sgu), g3(g_x), g3(g_mem), g3(g_ffb)
    g_final = g_final.reshape(1, d)

    reps = SGU_CHUNK // s_len
    ws_prompt = w_sgu_s
    ws_sample = jnp.tile(w_sgu_s[:, :, :s_len, :s_len], (1, 1, reps, reps))
    bias_prompt = jnp.repeat(jnp.swapaxes(b_sgu_s, 1, 2), SGU_GROUP_DIM, axis=2)
    bias_sample = jnp.tile(bias_prompt[:, :s_len], (1, reps, 1))

    pos_minor = lambda c: jnp.transpose(c, (0, 1, 3, 4, 2)).reshape(depth, c.shape[1], -1, c.shape[2])
    cache_kt, cache_vt = pos_minor(cache_k), pos_minor(cache_v)
    mem_kt, mem_vt = pos_minor(cache_mem_k), pos_minor(cache_mem_v)

    mk_p, mv_p = _memkv(mem_prompt.reshape(bp * n_mem, d), g_mem, w_xk, w_xv)
    mk_p4 = mk_p.reshape(depth, bp, n_mem, X_WIDTH)
    mv_p4 = mv_p.reshape(depth, bp, n_mem, X_WIDTH)

    xp = x_prompt.reshape(bp * t, d)
    xs = x_sample.reshape(bs * s_len, d)
    kp_l, vp_l, ks_l, vs_l, vrow_l = [], [], [], [], []
    for l in range(depth):
        last = l == depth - 1
        xp = _ffn(xp, l, g_ffa, w_ffa_in, w_ffa_out)
        qt, k, v, kb, vt, kmean, osgu = _inproj(
            xp, l, g_mix, w_in, ws_prompt, bias_prompt, g_sgu, prompt=True, chunk_len=SGU_CHUNK, batch=bp)
        kp_l.append(k)
        vp_l.append(v)
        oatt = _moba_prompt(qt, kb.reshape(bp, t, ATT_WIDTH), vt, kmean.reshape(bp, t // MOBA_BLOCK, ATT_WIDTH))
        xp = _outx_prompt(xp, oatt.reshape(bp * t, ATT_WIDTH), osgu, l, w_out, g_x, w_xq, mk_p4, mv_p4, w_xo, batch=bp)
        xp = _ffn(xp, l, g_ffb, w_ffb_in, w_ffb_out, final_g=g_final if last else None)
        xs = _ffn(xs, l, g_ffa, w_ffa_in, w_ffa_out)
        q, k, v, osgu, vn = _inproj(
            xs, l, g_mix, w_in, ws_sample, bias_sample, g_sgu, prompt=False, chunk_len=s_len, batch=bs)
        to3 = lambda a: a.reshape(bs, s_len, a.shape[-1])
        ks_l.append(k)
        vs_l.append(v)
        vrow_l.append(vn)
        oatt = _moba_sample(page_table, to3(q), to3(k), to3(v), cache_kt, cache_vt, l)
        x2, qx = _outq_sample(xs, oatt.reshape(bs * s_len, ATT_WIDTH), osgu, l, w_out, g_x, w_xq)
        ox = _xatt_sample(to3(qx), mem_kt, mem_vt, l)
        xs = _ffn(x2, l, g_ffb, w_ffb_in, w_ffb_out, pre=(ox.reshape(bs * s_len, X_WIDTH), w_xo),
                  final_g=g_final if last else None)

    y_prompt = xp.reshape(bp, t, d)
    y_sample = xs.reshape(bs, s_len, d)
    k_prompt = jnp.stack(kp_l).reshape(depth, bp, t // PAGE_SIZE, PAGE_SIZE, ATT_HEADS, HEAD_DIM)
    v_prompt = jnp.stack(vp_l).reshape(depth, bp, t // PAGE_SIZE, PAGE_SIZE, ATT_HEADS, HEAD_DIM)
    mem_k_prompt = mk_p.reshape(depth, bp, n_mem, X_HEADS, X_HEAD_DIM)
    mem_v_prompt = mv_p.reshape(depth, bp, n_mem, X_HEADS, X_HEAD_DIM)
    k_sample = jnp.stack(ks_l).reshape(depth, bs, s_len, ATT_HEADS, HEAD_DIM)
    v_sample = jnp.stack(vs_l).reshape(depth, bs, s_len, ATT_HEADS, HEAD_DIM)
    sgu_v_sample = jnp.stack(vrow_l).reshape(depth, bs, s_len, SGU_WIDTH)
    return (y_prompt, y_sample, k_prompt, v_prompt, mem_k_prompt, mem_v_prompt, k_sample, v_sample, sgu_v_sample)
```

```python
import functools

import numpy as np
import jax
import jax.numpy as jnp
from jax import lax
from jax.experimental import pallas as pl
from jax.experimental.pallas import tpu as pltpu

F32 = jnp.float32
BF16 = jnp.bfloat16

EPS = 1e-6
NEG = -1e30

ATT_HEADS = 8
HEAD_DIM = 64
ATT_WIDTH = ATT_HEADS * HEAD_DIM
MOBA_BLOCK = 256
MOBA_TOPK = 3
SGU_GROUPS = 4
SGU_GROUP_DIM = 128
SGU_CHUNK = 128
SGU_WIDTH = SGU_GROUPS * SGU_GROUP_DIM
X_HEADS = 4
X_HEAD_DIM = 64
X_WIDTH = X_HEADS * X_HEAD_DIM
PAGE_SIZE = 128

V7X_LANES = 128
V7X_VMEM_BYTES = 64 * 1024 * 1024
VMEM_LIMIT_BYTES = V7X_VMEM_BYTES - 8 * 1024 * 1024

ROW_TILE = 512
FFN_ROW_TILE = 1024
FF_CHUNK = 256

_ALIBI_SLOPES = tuple(float(2.0 ** (-8.0 * (h + 1) / ATT_HEADS)) for h in range(ATT_HEADS))
_HIGHEST = lax.Precision.HIGHEST
_LOG2E = float(np.log2(np.e))


def _params(*semantics):
    return pltpu.CompilerParams(dimension_semantics=semantics, vmem_limit_bytes=VMEM_LIMIT_BYTES)


def _resident(block, index_map):
    return pl.BlockSpec(block, index_map, pipeline_mode=pl.Buffered(1))


def _rms(x, g):
    return x * lax.rsqrt(jnp.mean(x * x, axis=-1, keepdims=True) + EPS) * g


def _dot(a, b):
    return jnp.dot(a, b, preferred_element_type=F32)


def _dot_nt(a, b, precision=None):
    return lax.dot_general(a, b, (((1,), (1,)), ((), ())), precision=precision, preferred_element_type=F32)


def _top_mask(gate, axis, n_valid, idx):
    n = gate.shape[axis]
    rank = jnp.zeros(gate.shape, jnp.int32)
    for other in range(n):
        g_o = lax.slice_in_dim(gate, other, other + 1, axis=axis)
        beats = (g_o > gate) | ((g_o == gate) & (other < idx))
        rank = rank + jnp.where(beats, jnp.where(other < n_valid, 1, 0), 0)
    return (rank < MOBA_TOPK) & (idx < n_valid)


def _ffn_kernel(*refs, pre_add, final_norm):
    it = iter(refs)
    x_ref = next(it)
    if pre_add:
        ox_ref, wxo_ref = next(it), next(it)
    g_ref, wa_ref, wb_ref, wo_ref = next(it), next(it), next(it), next(it)
    if final_norm:
        gf_ref = next(it)
    o_ref, s_ref = next(it), next(it)

    x = x_ref[...]
    if pre_add:
        x = x + _dot(ox_ref[...].astype(BF16), wxo_ref[...])
    h = _rms(x, g_ref[...]).astype(BF16)
    d_ff = wa_ref.shape[1]
    for c in range(d_ff // FF_CHUNK):
        cols = slice(c * FF_CHUNK, (c + 1) * FF_CHUNK)
        a = _dot(h, wa_ref[:, cols])
        b = _dot(h, wb_ref[:, cols])
        s_ref[:, cols] = (jax.nn.silu(a) * b).astype(BF16)
    y = x + 0.5 * _dot(s_ref[...], wo_ref[...])
    if final_norm:
        y = _rms(y, gf_ref[...])
    o_ref[...] = y


def _ffn(x, layer, g, w_in, w_out, *, pre=None, final_g=None):
    n, d = x.shape
    d_ff = w_out.shape[1]
    tm = min(FFN_ROW_TILE, n)
    assert d_ff % FF_CHUNK == 0 and n % tm == 0
    row = lambda i: (i, 0)
    args, specs = [x], [pl.BlockSpec((tm, d), row)]
    if pre is not None:
        ox, w_xo = pre
        args += [ox, w_xo]
        specs += [pl.BlockSpec((tm, ox.shape[1]), row), _resident((None,) + w_xo.shape[1:], lambda i: (layer, 0, 0))]
    args += [g, w_in, w_in, w_out]
    specs += [
        _resident((None, 1, d), lambda i: (layer, 0, 0)),
        _resident((None, d, d_ff), lambda i: (layer, 0, 0)),
        _resident((None, d, d_ff), lambda i: (layer, 0, 1)),
        _resident((None, d_ff, d), lambda i: (layer, 0, 0)),
    ]
    if final_g is not None:
        args.append(final_g)
        specs.append(_resident((1, d), lambda i: (0, 0)))
    return pl.pallas_call(
        functools.partial(_ffn_kernel, pre_add=pre is not None, final_norm=final_g is not None),
        out_shape=jax.ShapeDtypeStruct((n, d), F32),
        grid=(n // tm,),
        in_specs=specs,
        out_specs=pl.BlockSpec((tm, d), row),
        scratch_shapes=[pltpu.VMEM((tm, d_ff), BF16)],
        compiler_params=_params("parallel"),
        name="ffn",
    )(*args)


def _inproj_kernel(x_ref, g_ref, w_ref, ws_ref, bias_ref, gs_ref, *rest, prompt, chunk_len, n_carried):
    outs = rest[n_carried:]
    tm = x_ref.shape[0]
    h = _rms(x_ref[...], g_ref[...]).astype(BF16)
    aw = ATT_WIDTH
    zq = _dot(h, w_ref[:, 0:aw]) * (HEAD_DIM ** -0.5)
    if prompt:
        qt_ref, kt_ref, vtf_ref, kb_ref, vt_ref, km_ref, osgu_ref = outs
        qt_ref[...] = zq.T
        zk = _dot(h, w_ref[:, aw:2 * aw])
        kb_ref[...] = zk.astype(BF16)
        km_ref[...] = jnp.mean(zk.reshape(tm // MOBA_BLOCK, MOBA_BLOCK, aw), axis=1)
        zkt = zk.T
        zvt = _dot(h, w_ref[:, 2 * aw:3 * aw]).T
        vt_ref[...] = zvt.astype(BF16)
        for p in range(tm // PAGE_SIZE):
            lanes = slice(p * PAGE_SIZE, (p + 1) * PAGE_SIZE)
            kt_ref[p] = zkt[:, lanes]
            vtf_ref[p] = zvt[:, lanes]
    else:
        q_ref, k_ref, v_ref, osgu_ref, vn_ref = outs
        q_ref[...] = zq
        k_ref[...] = _dot(h, w_ref[:, aw:2 * aw])
        v_ref[...] = _dot(h, w_ref[:, 2 * aw:3 * aw])
    u = jax.nn.gelu(_dot(h, w_ref[:, 3 * aw:3 * aw + SGU_WIDTH]))
    vg = jax.nn.gelu(_dot(h, w_ref[:, 3 * aw + SGU_WIDTH:]))

    r = lax.broadcasted_iota(jnp.int32, (SGU_CHUNK, SGU_CHUNK), 0)
    c = lax.broadcasted_iota(jnp.int32, (SGU_CHUNK, SGU_CHUNK), 1)
    mix_mask = (r // chunk_len == c // chunk_len) & (c <= r)
    for grp in range(SGU_GROUPS):
        cols = slice(grp * SGU_GROUP_DIM, (grp + 1) * SGU_GROUP_DIM)
        vgg = vg[:, cols]
        vn = vgg * lax.rsqrt(jnp.mean(vgg * vgg, axis=-1, keepdims=True) + EPS) * gs_ref[:, cols]
        if not prompt:
            vn_ref[:, cols] = vn
        vnb = vn.astype(BF16)
        wsm = jnp.where(mix_mask, ws_ref[grp], 0.0).astype(BF16)
        for ch in range(tm // SGU_CHUNK):
            rows = slice(ch * SGU_CHUNK, (ch + 1) * SGU_CHUNK)
            mixed = _dot(wsm, vnb[rows]) + bias_ref[:, cols]
            osgu_ref[rows, cols] = (u[rows, cols] * mixed).astype(BF16)


def _inproj(x, layer, g, w_in, ws_tiled, bias_rows, g_sgu, *, prompt, chunk_len, batch, carried=()):
    n, d = x.shape
    tm = ROW_TILE
    assert n % tm == 0 and tm % MOBA_BLOCK == 0 and tm % SGU_CHUNK == 0
    row = lambda i: (i, 0)
    aw = ATT_WIDTH
    depth = w_in.shape[0]
    in_specs = [
        pl.BlockSpec((tm, d), row),
        _resident((None, 1, d), lambda i: (layer, 0, 0)),
        _resident((None, d, w_in.shape[2]), lambda i: (layer, 0, 0)),
        _resident((None, SGU_GROUPS, SGU_CHUNK, SGU_CHUNK), lambda i: (layer, 0, 0, 0)),
        _resident((None, SGU_CHUNK, SGU_WIDTH), lambda i: (layer, 0, 0)),
        _resident((None, 1, SGU_WIDTH), lambda i: (layer, 0, 0)),
    ] + [pl.BlockSpec(memory_space=pl.ANY)] * len(carried)
    aliases = {}
    if prompt:
        t = n // batch
        tiles_per_seq = t // tm
        pages = tm // PAGE_SIZE
        assert t % tm == 0 and tm % PAGE_SIZE == 0
        col = lambda i: (i // tiles_per_seq, 0, i % tiles_per_seq)
        paged = lambda i: (layer, i // tiles_per_seq, i % tiles_per_seq, 0, 0)
        kv_t = jax.ShapeDtypeStruct((depth, batch, t // PAGE_SIZE, aw, PAGE_SIZE), F32)
        out_shape = [
            jax.ShapeDtypeStruct((batch, aw, t), F32),
            kv_t,
            kv_t,
            jax.ShapeDtypeStruct((n, aw), BF16),
            jax.ShapeDtypeStruct((batch, aw, t), BF16),
            jax.ShapeDtypeStruct((n // tm, tm // MOBA_BLOCK, aw), F32),
            jax.ShapeDtypeStruct((n, SGU_WIDTH), BF16),
        ]
        out_specs = [
            pl.BlockSpec((None, aw, tm), col),
            pl.BlockSpec((None, None, pages, aw, PAGE_SIZE), paged),
            pl.BlockSpec((None, None, pages, aw, PAGE_SIZE), paged),
            pl.BlockSpec((tm, aw), row),
            pl.BlockSpec((None, aw, tm), col),
            pl.BlockSpec((None, tm // MOBA_BLOCK, aw), lambda i: (i, 0, 0)),
            pl.BlockSpec((tm, SGU_WIDTH), row),
        ]
        if carried:
            aliases = {len(in_specs) - 2: 1, len(in_specs) - 1: 2}
    else:
        out_shape = [
            jax.ShapeDtypeStruct((n, aw), F32),
            jax.ShapeDtypeStruct((n, aw), F32),
            jax.ShapeDtypeStruct((n, aw), F32),
            jax.ShapeDtypeStruct((n, SGU_WIDTH), BF16),
            jax.ShapeDtypeStruct((n, SGU_WIDTH), F32),
        ]
        out_specs = [pl.BlockSpec((tm, aw), row)] * 3 + [pl.BlockSpec((tm, SGU_WIDTH), row)] * 2
    return pl.pallas_call(
        functools.partial(_inproj_kernel, prompt=prompt, chunk_len=chunk_len, n_carried=len(carried)),
        out_shape=out_shape,
        grid=(n // tm,),
        in_specs=in_specs,
        out_specs=out_specs,
        input_output_aliases=aliases,
        compiler_params=_params("parallel"),
        name="inproj_prompt" if prompt else "inproj_sample",
    )(x, g, w_in, ws_tiled, bias_rows, g_sgu, *carried)


def _moba_prompt_kernel(qt_ref, kb_ref, vt_ref, km_ref, o_ref, base_ref, bias_ref, qmb_ref, sc_ref, acc_ref):
    blk = MOBA_BLOCK
    i = pl.program_id(1)
    n_blocks = km_ref.shape[0]
    pair_w = 2 * HEAD_DIM
    pair_row = lax.broadcasted_iota(jnp.int32, (pair_w, blk), 0)
    blk_idx = lax.broadcasted_iota(jnp.int32, (n_blocks, blk), 0)
    pairs = [slice((h // 2) * pair_w, (h // 2 + 1) * pair_w) for h in range(ATT_HEADS)]
    heads = [slice(h * HEAD_DIM, (h + 1) * HEAD_DIM) for h in range(ATT_HEADS)]
    slopes2 = [s * _LOG2E for s in _ALIBI_SLOPES]

    @pl.when(i == 0)
    def _():
        rk = lax.broadcasted_iota(jnp.int32, (blk, blk), 0)
        rq = lax.broadcasted_iota(jnp.int32, (blk, blk), 1)
        dist = (rq - rk).astype(F32)
        for h in range(ATT_HEADS):
            base_ref[0, h] = -slopes2[h] * dist
            base_ref[1, h] = jnp.where(rq >= rk, -slopes2[h] * dist, NEG)

    for h in range(ATT_HEADS):
        qm = jnp.where(pair_row // HEAD_DIM == h % 2, qt_ref[pairs[h], :], 0.0)
        qmb_ref[h] = (qm * _LOG2E).astype(BF16)
        gate = jnp.dot(km_ref[:, pairs[h]], qm, precision=_HIGHEST, preferred_element_type=F32)
        sel = _top_mask(gate, 0, i, blk_idx)
        bias_ref[h * n_blocks:(h + 1) * n_blocks, :] = jnp.where(
            sel | (blk_idx == i), (-slopes2[h] * blk) * (i - blk_idx).astype(F32), NEG)
    acc_ref[...] = jnp.zeros(acc_ref.shape, F32)

    def score_body(j, maxima):
        start = pl.multiple_of(j * blk, blk)
        is_own = (j == i).astype(jnp.int32)
        out = []
        for h in range(ATT_HEADS):
            s = (_dot(kb_ref[pl.ds(start, blk), pairs[h]], qmb_ref[h]) + base_ref[is_own, h]
                 + bias_ref[pl.ds(h * n_blocks + j, 1), :])
            sc_ref[h, j] = s
            out.append(jnp.maximum(maxima[h], jnp.max(s, axis=0, keepdims=True)))
        return tuple(out)

    maxima = lax.fori_loop(0, i + 1, score_body, tuple(jnp.full((1, blk), NEG, F32) for _ in range(ATT_HEADS)))

    def value_body(j, sums):
        start = pl.multiple_of(j * blk, blk)
        out = []
        for h in range(ATT_HEADS):
            p = jnp.exp2(sc_ref[h, j] - maxima[h])
            out.append(sums[h] + jnp.sum(p, axis=0, keepdims=True))
            acc_ref[heads[h], :] += _dot(vt_ref[heads[h], pl.ds(start, blk)], p.astype(BF16))
        return tuple(out)

    sums = lax.fori_loop(0, i + 1, value_body, tuple(jnp.zeros((1, blk), F32) for _ in range(ATT_HEADS)))
    outs = [acc_ref[heads[h], :] / sums[h] for h in range(ATT_HEADS)]
    o_ref[...] = jnp.concatenate(outs, axis=0).T.astype(BF16)


def _moba_prompt(qt, kb, vt, kmean):
    batch, aw, t = qt.shape
    n_blocks = t // MOBA_BLOCK
    assert t % MOBA_BLOCK == 0 and kmean.shape == (batch, n_blocks, aw)
    return pl.pallas_call(
        _moba_prompt_kernel,
        out_shape=jax.ShapeDtypeStruct((batch, t, aw), BF16),
        grid=(batch, n_blocks),
        in_specs=[
            pl.BlockSpec((None, aw, MOBA_BLOCK), lambda b, i: (b, 0, i)),
            _resident((None, t, aw), lambda b, i: (b, 0, 0)),
            _resident((None, aw, t), lambda b, i: (b, 0, 0)),
            _resident((None, n_blocks, aw), lambda b, i: (b, 0, 0)),
        ],
        out_specs=pl.BlockSpec((None, MOBA_BLOCK, aw), lambda b, i: (b, i, 0)),
        scratch_shapes=[
            pltpu.VMEM((2, ATT_HEADS, MOBA_BLOCK, MOBA_BLOCK), F32),
            pltpu.VMEM((ATT_HEADS * n_blocks, MOBA_BLOCK), F32),
            pltpu.VMEM((ATT_HEADS, 2 * HEAD_DIM, MOBA_BLOCK), BF16),
            pltpu.VMEM((ATT_HEADS, n_blocks, MOBA_BLOCK, MOBA_BLOCK), F32),
            pltpu.VMEM((ATT_WIDTH, MOBA_BLOCK), F32),
        ],
        compiler_params=_params("parallel", "arbitrary"),
        name="moba_prompt",
    )(qt, kb, vt, kmean)


_ROWS_PER_QUERY = 8


def _block_diag_queries(q, head_dim):
    s_len, w = q.shape
    lane_head = lax.broadcasted_iota(jnp.int32, (_ROWS_PER_QUERY, w), 1) // head_dim
    row_head = lax.broadcasted_iota(jnp.int32, (_ROWS_PER_QUERY, w), 0)
    keep = lane_head == row_head
    parts = [jnp.where(keep, jnp.broadcast_to(q[s:s + 1, :], (_ROWS_PER_QUERY, w)), 0.0) for s in range(s_len)]
    return jnp.concatenate(parts, axis=0), keep


def _pick_block_diag(r, keep, s_len):
    w = r.shape[1]
    keep_all = jnp.concatenate([keep] * s_len, axis=0)
    return jnp.sum(jnp.where(keep_all, r, 0.0).reshape(s_len, _ROWS_PER_QUERY, w), axis=1)


def _moba_sample_kernel(pt_ref, q_ref, ko_ref, vo_ref, *rest, n_pages, past_len):
    del pt_ref
    k_pages, v_pages, o_ref = rest[:n_pages], rest[n_pages:2 * n_pages], rest[2 * n_pages]
    s_len, aw = q_ref.shape
    blk = MOBA_BLOCK
    pages_per_block = blk // PAGE_SIZE
    n_blocks = past_len // blk
    n_rows = s_len * _ROWS_PER_QUERY

    q = q_ref[...]
    qbd, keep = _block_diag_queries(q, HEAD_DIM)
    qbd_b = qbd.astype(BF16)
    row = lax.broadcasted_iota(jnp.int32, (n_rows, 1), 0)
    row_head = row % _ROWS_PER_QUERY
    slope = jnp.zeros((n_rows, 1), F32)
    for h in range(ATT_HEADS):
        slope = jnp.where(row_head == h, _ALIBI_SLOPES[h], slope)
    qpos = past_len + row // _ROWS_PER_QUERY

    scores = []
    blk_lane = lax.broadcasted_iota(jnp.int32, (aw, n_blocks), 1)
    kmean_t = jnp.zeros((aw, n_blocks), F32)
    for n in range(n_blocks):
        ksum = jnp.zeros((aw, PAGE_SIZE), F32)
        for t in range(pages_per_block):
            kp = k_pages[n * pages_per_block + t][...]
            ksum = ksum + kp
            scores.append(_dot(qbd_b, kp.astype(BF16)))
        kmean_t = jnp.where(blk_lane == n, jnp.sum(ksum, axis=1, keepdims=True) * (1.0 / blk), kmean_t)
    q_t = jnp.concatenate([q, jnp.zeros((_ROWS_PER_QUERY - s_len, aw), F32)], axis=0).T
    gate = jnp.concatenate(
        [jnp.sum((q_t[:, s:s + 1] * kmean_t).reshape(ATT_HEADS, HEAD_DIM, n_blocks), axis=1) for s in range(s_len)],
        axis=0)
    sel = _top_mask(gate, 1, n_blocks, lax.broadcasted_iota(jnp.int32, (n_rows, n_blocks), 1))
    unselected = jnp.where(sel, 0.0, NEG)

    key_off = lax.broadcasted_iota(jnp.int32, (n_rows, PAGE_SIZE), 1)
    for pg in range(n_pages):
        n = pg // pages_per_block
        dist = (qpos - (pg * PAGE_SIZE + key_off)).astype(F32)
        scores[pg] = scores[pg] - slope * dist + unselected[:, n:n + 1]
    k_own, v_own = ko_ref[...], vo_ref[...]
    own = []
    for t in range(s_len):
        s_t = jnp.sum(qbd * k_own[t:t + 1, :], axis=1, keepdims=True)
        d_t = qpos - (past_len + t)
        own.append(jnp.where(d_t >= 0, s_t - slope * d_t.astype(F32), NEG))

    m_all = scores[0]
    for s_pg in scores[1:]:
        m_all = jnp.maximum(m_all, s_pg)
    m = jnp.max(m_all, axis=1, keepdims=True)
    for s_t in own:
        m = jnp.maximum(m, s_t)
    l = jnp.zeros((n_rows, 1), F32)
    acc = jnp.zeros((n_rows, aw), F32)
    for t in range(s_len):
        p_t = jnp.exp(own[t] - m)
        l = l + p_t
        acc = acc + p_t * v_own[t:t + 1, :]
    p_sum = jnp.zeros((n_rows, PAGE_SIZE), F32)
    for pg in range(n_pages):
        p_pg = jnp.exp(scores[pg] - m)
        p_sum = p_sum + p_pg
        acc = acc + _dot_nt(p_pg.astype(BF16), v_pages[pg][...].astype(BF16))
    l = l + jnp.sum(p_sum, axis=1, keepdims=True)
    o_ref[...] = _pick_block_diag(acc / l, keep, s_len)


def _moba_sample(page_table, q3, k3, v3, cache_kt, cache_vt, layer):
    n_req, s_len, aw = q3.shape
    n_pages = page_table.shape[1]
    past_len = n_pages * PAGE_SIZE
    assert past_len % MOBA_BLOCK == 0 and s_len <= _ROWS_PER_QUERY and MOBA_BLOCK % PAGE_SIZE == 0
    assert past_len // MOBA_BLOCK >= MOBA_TOPK and cache_kt.shape[2:] == (aw, PAGE_SIZE)
    new = pl.BlockSpec((None, s_len, aw), lambda b, pt: (b, 0, 0))
    page_specs = [
        pl.BlockSpec((None, None, aw, PAGE_SIZE), lambda b, pt, p=p: (layer, pt[b, p], 0, 0)) for p in range(n_pages)
    ]
    return pl.pallas_call(
        functools.partial(_moba_sample_kernel, n_pages=n_pages, past_len=past_len),
        out_shape=jax.ShapeDtypeStruct((n_req, s_len, aw), F32),
        grid_spec=pltpu.PrefetchScalarGridSpec(
            num_scalar_prefetch=1,
            grid=(n_req,),
            in_specs=[new, new, new] + page_specs + page_specs,
            out_specs=new,
        ),
        compiler_params=_params("parallel"),
        name="moba_sample",
    )(page_table, q3, k3, v3, *([cache_kt] * n_pages), *([cache_vt] * n_pages))


def _mix_out(x_ref, oatt_ref, osgu_ref, wo_ref):
    return (x_ref[...] + _dot(oatt_ref[...].astype(BF16), wo_ref[0:ATT_WIDTH, :])
            + _dot(osgu_ref[...], wo_ref[ATT_WIDTH:, :]))


def _outx_prompt_kernel(x_ref, oatt_ref, osgu_ref, wo_ref, gx_ref, wxq_ref, mk_ref, mv_ref, wxo_ref, o_ref):
    tm = x_ref.shape[0]
    x2 = _mix_out(x_ref, oatt_ref, osgu_ref, wo_ref)
    qx = _dot(_rms(x2, gx_ref[...]).astype(BF16), wxq_ref[...]) * (X_HEAD_DIM ** -0.5)
    mk = mk_ref[...].astype(BF16)
    mv = mv_ref[...].astype(BF16)
    pair_w = 2 * X_HEAD_DIM
    lane = lax.broadcasted_iota(jnp.int32, (tm, pair_w), 1)
    pieces = []
    for hp in range(X_HEADS // 2):
        pair = slice(hp * pair_w, (hp + 1) * pair_w)
        q_pair = qx[:, pair]
        res = []
        for sub in range(2):
            qm = jnp.where(lane // X_HEAD_DIM == sub, q_pair, 0.0).astype(BF16)
            s = _dot_nt(qm, mk[:, pair])
            p = jnp.exp(s - jnp.max(s, axis=1, keepdims=True))
            p = p / jnp.sum(p, axis=1, keepdims=True)
            res.append(_dot(p.astype(BF16), mv[:, pair]))
        pieces.append(jnp.where(lane // X_HEAD_DIM == 0, res[0], res[1]))
    ox = jnp.concatenate(pieces, axis=1).astype(BF16)
    o_ref[...] = x2 + _dot(ox, wxo_ref[...])


def _outx_prompt(x, oatt, osgu, layer, w_out, g_x, w_xq, mem_k, mem_v, w_xo, *, batch):
    n, d = x.shape
    tm = ROW_TILE
    tiles_per_seq = n // batch // tm
    n_mem = mem_k.shape[2]
    row = lambda i: (i, 0)
    mem = pl.BlockSpec((None, None, n_mem, X_WIDTH), lambda i: (layer, i // tiles_per_seq, 0, 0))
    return pl.pallas_call(
        _outx_prompt_kernel,
        out_shape=jax.ShapeDtypeStruct((n, d), F32),
        grid=(n // tm,),
        in_specs=[
            pl.BlockSpec((tm, d), row),
            pl.BlockSpec((tm, ATT_WIDTH), row),
            pl.BlockSpec((tm, SGU_WIDTH), row),
            _resident((None,) + w_out.shape[1:], lambda i: (layer, 0, 0)),
            _resident((None, 1, d), lambda i: (layer, 0, 0)),
            _resident((None, d, X_WIDTH), lambda i: (layer, 0, 0)),
            mem,
            mem,
            _resident((None, X_WIDTH, d), lambda i: (layer, 0, 0)),
        ],
        out_specs=pl.BlockSpec((tm, d), row),
        compiler_params=_params("parallel"),
        name="outx_prompt",
    )(x, oatt, osgu, w_out, g_x, w_xq, mem_k, mem_v, w_xo)


def _outq_sample_kernel(x_ref, oatt_ref, osgu_ref, wo_ref, gx_ref, wxq_ref, x2_ref, q_ref):
    x2 = _mix_out(x_ref, oatt_ref, osgu_ref, wo_ref)
    x2_ref[...] = x2
    q_ref[...] = _dot(_rms(x2, gx_ref[...]).astype(BF16), wxq_ref[...]) * (X_HEAD_DIM ** -0.5)


def _outq_sample(x, oatt, osgu, layer, w_out, g_x, w_xq):
    n, d = x.shape
    tm = ROW_TILE
    row = lambda i: (i, 0)
    return pl.pallas_call(
        _outq_sample_kernel,
        out_shape=[jax.ShapeDtypeStruct((n, d), F32), jax.ShapeDtypeStruct((n, X_WIDTH), F32)],
        grid=(n // tm,),
        in_specs=[
            pl.BlockSpec((tm, d), row),
            pl.BlockSpec((tm, ATT_WIDTH), row),
            pl.BlockSpec((tm, SGU_WIDTH), row),
            _resident((None,) + w_out.shape[1:], lambda i: (layer, 0, 0)),
            _resident((None, 1, d), lambda i: (layer, 0, 0)),
            _resident((None, d, X_WIDTH), lambda i: (layer, 0, 0)),
        ],
        out_specs=[pl.BlockSpec((tm, d), row), pl.BlockSpec((tm, X_WIDTH), row)],
        compiler_params=_params("parallel"),
        name="outq_sample",
    )(x, oatt, osgu, w_out, g_x, w_xq)


_XATT_REQ_TILE = 8


def _xatt_sample_kernel(q_ref, mkt_ref, mvt_ref, o_ref):
    s_len = q_ref.shape[1]
    for b in range(q_ref.shape[0]):
        qbd, keep = _block_diag_queries(q_ref[b], X_HEAD_DIM)
        s = _dot(qbd.astype(BF16), mkt_ref[b].astype(BF16))
        p = jnp.exp(s - jnp.max(s, axis=1, keepdims=True))
        p = p / jnp.sum(p, axis=1, keepdims=True)
        r = _dot_nt(p.astype(BF16), mvt_ref[b].astype(BF16))
        o_ref[b] = _pick_block_diag(r, keep, s_len)


def _xatt_sample(q3, mem_kt, mem_vt, layer):
    n_req, s_len, _ = q3.shape
    n_mem = mem_kt.shape[3]
    bt = _XATT_REQ_TILE
    assert n_req % bt == 0 and mem_kt.shape[2] == X_WIDTH
    mem = pl.BlockSpec((None, bt, X_WIDTH, n_mem), lambda i: (layer, i, 0, 0))
    blk = pl.BlockSpec((bt, s_len, X_WIDTH), lambda i: (i, 0, 0))
    return pl.pallas_call(
        _xatt_sample_kernel,
        out_shape=jax.ShapeDtypeStruct((n_req, s_len, X_WIDTH), F32),
        grid=(n_req // bt,),
        in_specs=[blk, mem, mem],
        out_specs=blk,
        compiler_params=_params("parallel"),
        name="xatt_sample",
    )(q3, mem_kt, mem_vt)


def _memkv_kernel(mem_ref, g_ref, wk_ref, wv_ref, mk_ref, mv_ref):
    hm = _rms(mem_ref[...], g_ref[...]).astype(BF16)
    mk_ref[...] = _dot(hm, wk_ref[...])
    mv_ref[...] = _dot(hm, wv_ref[...])


def _memkv(mem, g_mem, w_xk, w_xv):
    n, d = mem.shape
    depth = g_mem.shape[0]
    tm = min(ROW_TILE, n)
    assert n % tm == 0
    w = pl.BlockSpec((None, d, X_WIDTH), lambda l, i: (l, 0, 0))
    out = pl.BlockSpec((None, tm, X_WIDTH), lambda l, i: (l, i, 0))
    return pl.pallas_call(
        _memkv_kernel,
        out_shape=[jax.ShapeDtypeStruct((depth, n, X_WIDTH), F32)] * 2,
        grid=(depth, n // tm),
        in_specs=[pl.BlockSpec((tm, d), lambda l, i: (i, 0)), pl.BlockSpec((None, 1, d), lambda l, i: (l, 0, 0)), w, w],
        out_specs=[out, out],
        compiler_params=_params("parallel", "parallel"),
        name="memkv",
    )(mem, g_mem, w_xk, w_xv)


def kernel(x_prompt, x_sample, cache_k, cache_v, cache_mem_k, cache_mem_v, page_table, mem_prompt, g_ffa, w_ffa_in, w_ffa_out, g_mix, w_in, w_out, w_sgu_s, b_sgu_s, g_sgu, g_x, w_xq, g_mem, w_xk, w_xv, w_xo, g_ffb, w_ffb_in, w_ffb_out, g_final):
    bp, t, d = x_prompt.shape
    bs, s_len, _ = x_sample.shape
    depth = g_ffa.shape[0]
    n_mem = mem_prompt.shape[1]
    assert t % SGU_CHUNK == 0 and SGU_CHUNK % s_len == 0 and s_len < SGU_CHUNK

    bf = lambda w: w.astype(BF16)
    w_ffa_in, w_ffa_out, w_ffb_in, w_ffb_out = bf(w_ffa_in), bf(w_ffa_out), bf(w_ffb_in), bf(w_ffb_out)
    w_in, w_out, w_xq, w_xk, w_xv, w_xo = bf(w_in), bf(w_out), bf(w_xq), bf(w_xk), bf(w_xv), bf(w_xo)
    g3 = lambda g: g.reshape(depth, 1, g.shape[-1])
    g_ffa, g_mix, g_sgu, g_x, g_mem, g_ffb = g3(g_ffa), g3(g_mix), g3(g_sgu), g3(g_x), g3(g_mem), g3(g_ffb)
    g_final = g_final.reshape(1, d)

    reps = SGU_CHUNK // s_len
    ws_prompt = w_sgu_s
    ws_sample = jnp.tile(w_sgu_s[:, :, :s_len, :s_len], (1, 1, reps, reps))
    bias_prompt = jnp.repeat(jnp.swapaxes(b_sgu_s, 1, 2), SGU_GROUP_DIM, axis=2)
    bias_sample = jnp.tile(bias_prompt[:, :s_len], (1, reps, 1))

    pos_minor = lambda c: jnp.transpose(c, (0, 1, 3, 4, 2)).reshape(depth, c.shape[1], -1, c.shape[2])
    cache_kt, cache_vt = pos_minor(cache_k), pos_minor(cache_v)
    mem_kt, mem_vt = pos_minor(cache_mem_k), pos_minor(cache_mem_v)

    mk_p, mv_p = _memkv(mem_prompt.reshape(bp * n_mem, d), g_mem, w_xk, w_xv)
    mk_p4 = mk_p.reshape(depth, bp, n_mem, X_WIDTH)
    mv_p4 = mv_p.reshape(depth, bp, n_mem, X_WIDTH)

    xp = x_prompt.reshape(bp * t, d)
    xs = x_sample.reshape(bs * s_len, d)
    ks_l, vs_l, vrow_l = [], [], []
    kv_t = ()
    for l in range(depth):
        last = l == depth - 1
        xp = _ffn(xp, l, g_ffa, w_ffa_in, w_ffa_out)
        qt, kt_all, vt_all, kb, vt, kmean, osgu = _inproj(
            xp, l, g_mix, w_in, ws_prompt, bias_prompt, g_sgu, prompt=True, chunk_len=SGU_CHUNK, batch=bp,
            carried=kv_t)
        kv_t = (kt_all, vt_all)
        oatt = _moba_prompt(qt, kb.reshape(bp, t, ATT_WIDTH), vt, kmean.reshape(bp, t // MOBA_BLOCK, ATT_WIDTH))
        xp = _outx_prompt(xp, oatt.reshape(bp * t, ATT_WIDTH), osgu, l, w_out, g_x, w_xq, mk_p4, mv_p4, w_xo, batch=bp)
        xp = _ffn(xp, l, g_ffb, w_ffb_in, w_ffb_out, final_g=g_final if last else None)
        xs = _ffn(xs, l, g_ffa, w_ffa_in, w_ffa_out)
        q, k, v, osgu, vn = _inproj(
            xs, l, g_mix, w_in, ws_sample, bias_sample, g_sgu, prompt=False, chunk_len=s_len, batch=bs)
        to3 = lambda a: a.reshape(bs, s_len, a.shape[-1])
        ks_l.append(k)
        vs_l.append(v)
        vrow_l.append(vn)
        oatt = _moba_sample(page_table, to3(q), to3(k), to3(v), cache_kt, cache_vt, l)
        x2, qx = _outq_sample(xs, oatt.reshape(bs * s_len, ATT_WIDTH), osgu, l, w_out, g_x, w_xq)
        ox = _xatt_sample(to3(qx), mem_kt, mem_vt, l)
        xs = _ffn(x2, l, g_ffb, w_ffb_in, w_ffb_out, pre=(ox.reshape(bs * s_len, X_WIDTH), w_xo),
                  final_g=g_final if last else None)

    y_prompt = xp.reshape(bp, t, d)
    y_sample = xs.reshape(bs, s_len, d)
    paged = lambda a: jnp.transpose(
        a.reshape(depth, bp, t // PAGE_SIZE, ATT_HEADS, HEAD_DIM, PAGE_SIZE), (0, 1, 2, 5, 3, 4))
    k_prompt, v_prompt = paged(kv_t[0]), paged(kv_t[1])
    mem_k_prompt = mk_p.reshape(depth, bp, n_mem, X_HEADS, X_HEAD_DIM)
    mem_v_prompt = mv_p.reshape(depth, bp, n_mem, X_HEADS, X_HEAD_DIM)
    k_sample = jnp.stack(ks_l).reshape(depth, bs, s_len, ATT_HEADS, HEAD_DIM)
    v_sample = jnp.stack(vs_l).reshape(depth, bs, s_len, ATT_HEADS, HEAD_DIM)
    sgu_v_sample = jnp.stack(vrow_l).reshape(depth, bs, s_len, SGU_WIDTH)
    return (y_prompt, y_sample, k_prompt, v_prompt, mem_k_prompt, mem_v_prompt, k_sample, v_sample, sgu_v_sample)
```

```python
import functools

import numpy as np
import jax
import jax.numpy as jnp
from jax import lax
from jax.experimental import pallas as pl
from jax.experimental.pallas import tpu as pltpu

F32 = jnp.float32
BF16 = jnp.bfloat16

EPS = 1e-6
NEG = -1e30

ATT_HEADS = 8
HEAD_DIM = 64
ATT_WIDTH = ATT_HEADS * HEAD_DIM
MOBA_BLOCK = 256
MOBA_TOPK = 3
SGU_GROUPS = 4
SGU_GROUP_DIM = 128
SGU_CHUNK = 128
SGU_WIDTH = SGU_GROUPS * SGU_GROUP_DIM
X_HEADS = 4
X_HEAD_DIM = 64
X_WIDTH = X_HEADS * X_HEAD_DIM
PAGE_SIZE = 128

V7X_LANES = 128
V7X_VMEM_BYTES = 64 * 1024 * 1024
VMEM_LIMIT_BYTES = V7X_VMEM_BYTES - 8 * 1024 * 1024

ROW_TILE = 512
FFN_ROW_TILE = 1024
FF_CHUNK = 256

_ALIBI_SLOPES = tuple(float(2.0 ** (-8.0 * (h + 1) / ATT_HEADS)) for h in range(ATT_HEADS))
_HIGHEST = lax.Precision.HIGHEST
_LOG2E = float(np.log2(np.e))
_DENOM_ROWS = 16


def _params(*semantics):
    return pltpu.CompilerParams(dimension_semantics=semantics, vmem_limit_bytes=VMEM_LIMIT_BYTES)


def _resident(block, index_map):
    return pl.BlockSpec(block, index_map, pipeline_mode=pl.Buffered(1))


def _rms(x, g):
    return x * lax.rsqrt(jnp.mean(x * x, axis=-1, keepdims=True) + EPS) * g


def _dot(a, b):
    return jnp.dot(a, b, preferred_element_type=F32)


def _dot_nt(a, b, precision=None):
    return lax.dot_general(a, b, (((1,), (1,)), ((), ())), precision=precision, preferred_element_type=F32)


def _top_mask(gate, axis, n_valid, idx):
    n = gate.shape[axis]
    rank = jnp.zeros(gate.shape, jnp.int32)
    for other in range(n):
        g_o = lax.slice_in_dim(gate, other, other + 1, axis=axis)
        beats = (g_o > gate) | ((g_o == gate) & (other < idx))
        rank = rank + jnp.where(beats, jnp.where(other < n_valid, 1, 0), 0)
    return (rank < MOBA_TOPK) & (idx < n_valid)


def _ffn_kernel(*refs, pre_add, final_norm):
    it = iter(refs)
    x_ref = next(it)
    if pre_add:
        ox_ref, wxo_ref = next(it), next(it)
    g_ref, wa_ref, wb_ref, wo_ref = next(it), next(it), next(it), next(it)
    if final_norm:
        gf_ref = next(it)
    o_ref, s_ref = next(it), next(it)

    x = x_ref[...]
    if pre_add:
        x = x + _dot(ox_ref[...].astype(BF16), wxo_ref[...])
    h = _rms(x, g_ref[...]).astype(BF16)
    d_ff = wa_ref.shape[1]
    for c in range(d_ff // FF_CHUNK):
        cols = slice(c * FF_CHUNK, (c + 1) * FF_CHUNK)
        a = _dot(h, wa_ref[:, cols])
        b = _dot(h, wb_ref[:, cols])
        s_ref[:, cols] = (jax.nn.silu(a) * b).astype(BF16)
    y = x + 0.5 * _dot(s_ref[...], wo_ref[...])
    if final_norm:
        y = _rms(y, gf_ref[...])
    o_ref[...] = y


def _ffn(x, layer, g, w_in, w_out, *, pre=None, final_g=None):
    n, d = x.shape
    d_ff = w_out.shape[1]
    tm = min(FFN_ROW_TILE, n)
    assert d_ff % FF_CHUNK == 0 and n % tm == 0
    row = lambda i: (i, 0)
    args, specs = [x], [pl.BlockSpec((tm, d), row)]
    if pre is not None:
        ox, w_xo = pre
        args += [ox, w_xo]
        specs += [pl.BlockSpec((tm, ox.shape[1]), row), _resident((None,) + w_xo.shape[1:], lambda i: (layer, 0, 0))]
    args += [g, w_in, w_in, w_out]
    specs += [
        _resident((None, 1, d), lambda i: (layer, 0, 0)),
        _resident((None, d, d_ff), lambda i: (layer, 0, 0)),
        _resident((None, d, d_ff), lambda i: (layer, 0, 1)),
        _resident((None, d_ff, d), lambda i: (layer, 0, 0)),
    ]
    if final_g is not None:
        args.append(final_g)
        specs.append(_resident((1, d), lambda i: (0, 0)))
    return pl.pallas_call(
        functools.partial(_ffn_kernel, pre_add=pre is not None, final_norm=final_g is not None),
        out_shape=jax.ShapeDtypeStruct((n, d), F32),
        grid=(n // tm,),
        in_specs=specs,
        out_specs=pl.BlockSpec((tm, d), row),
        scratch_shapes=[pltpu.VMEM((tm, d_ff), BF16)],
        compiler_params=_params("parallel"),
        name="ffn",
    )(*args)


def _inproj_kernel(x_ref, g_ref, w_ref, ws_ref, bias_ref, gs_ref, *rest, prompt, chunk_len, n_carried):
    outs = rest[n_carried:]
    tm = x_ref.shape[0]
    h = _rms(x_ref[...], g_ref[...]).astype(BF16)
    aw = ATT_WIDTH
    zq = _dot(h, w_ref[:, 0:aw]) * (HEAD_DIM ** -0.5)
    if prompt:
        qt_ref, kt_ref, vtf_ref, kb_ref, vt_ref, km_ref, osgu_ref = outs
        qt_ref[...] = zq.T
        zk = _dot(h, w_ref[:, aw:2 * aw])
        kb_ref[...] = zk.astype(BF16)
        km_ref[...] = jnp.mean(zk.reshape(tm // MOBA_BLOCK, MOBA_BLOCK, aw), axis=1)
        zkt = zk.T
        zvt = _dot(h, w_ref[:, 2 * aw:3 * aw]).T
        vt_ref[...] = zvt.astype(BF16)
        for p in range(tm // PAGE_SIZE):
            lanes = slice(p * PAGE_SIZE, (p + 1) * PAGE_SIZE)
            kt_ref[p] = zkt[:, lanes]
            vtf_ref[p] = zvt[:, lanes]
    else:
        q_ref, k_ref, v_ref, osgu_ref, vn_ref = outs
        q_ref[...] = zq
        k_ref[...] = _dot(h, w_ref[:, aw:2 * aw])
        v_ref[...] = _dot(h, w_ref[:, 2 * aw:3 * aw])
    u = jax.nn.gelu(_dot(h, w_ref[:, 3 * aw:3 * aw + SGU_WIDTH]))
    vg = jax.nn.gelu(_dot(h, w_ref[:, 3 * aw + SGU_WIDTH:]))

    r = lax.broadcasted_iota(jnp.int32, (SGU_CHUNK, SGU_CHUNK), 0)
    c = lax.broadcasted_iota(jnp.int32, (SGU_CHUNK, SGU_CHUNK), 1)
    mix_mask = (r // chunk_len == c // chunk_len) & (c <= r)
    for grp in range(SGU_GROUPS):
        cols = slice(grp * SGU_GROUP_DIM, (grp + 1) * SGU_GROUP_DIM)
        vgg = vg[:, cols]
        vn = vgg * lax.rsqrt(jnp.mean(vgg * vgg, axis=-1, keepdims=True) + EPS) * gs_ref[:, cols]
        if not prompt:
            vn_ref[:, cols] = vn
        vnb = vn.astype(BF16)
        wsm = jnp.where(mix_mask, ws_ref[grp], 0.0).astype(BF16)
        for ch in range(tm // SGU_CHUNK):
            rows = slice(ch * SGU_CHUNK, (ch + 1) * SGU_CHUNK)
            mixed = _dot(wsm, vnb[rows]) + bias_ref[:, cols]
            osgu_ref[rows, cols] = (u[rows, cols] * mixed).astype(BF16)


def _inproj(x, layer, g, w_in, ws_tiled, bias_rows, g_sgu, *, prompt, chunk_len, batch, carried=()):
    n, d = x.shape
    tm = ROW_TILE
    assert n % tm == 0 and tm % MOBA_BLOCK == 0 and tm % SGU_CHUNK == 0
    row = lambda i: (i, 0)
    aw = ATT_WIDTH
    depth = w_in.shape[0]
    in_specs = [
        pl.BlockSpec((tm, d), row),
        _resident((None, 1, d), lambda i: (layer, 0, 0)),
        _resident((None, d, w_in.shape[2]), lambda i: (layer, 0, 0)),
        _resident((None, SGU_GROUPS, SGU_CHUNK, SGU_CHUNK), lambda i: (layer, 0, 0, 0)),
        _resident((None, SGU_CHUNK, SGU_WIDTH), lambda i: (layer, 0, 0)),
        _resident((None, 1, SGU_WIDTH), lambda i: (layer, 0, 0)),
    ] + [pl.BlockSpec(memory_space=pl.ANY)] * len(carried)
    aliases = {}
    if prompt:
        t = n // batch
        tiles_per_seq = t // tm
        pages = tm // PAGE_SIZE
        assert t % tm == 0 and tm % PAGE_SIZE == 0
        col = lambda i: (i // tiles_per_seq, 0, i % tiles_per_seq)
        paged = lambda i: (layer, i // tiles_per_seq, i % tiles_per_seq, 0, 0)
        kv_t = jax.ShapeDtypeStruct((depth, batch, t // PAGE_SIZE, aw, PAGE_SIZE), F32)
        out_shape = [
            jax.ShapeDtypeStruct((batch, aw, t), F32),
            kv_t,
            kv_t,
            jax.ShapeDtypeStruct((n, aw), BF16),
            jax.ShapeDtypeStruct((batch, aw, t), BF16),
            jax.ShapeDtypeStruct((n // tm, tm // MOBA_BLOCK, aw), F32),
            jax.ShapeDtypeStruct((n, SGU_WIDTH), BF16),
        ]
        out_specs = [
            pl.BlockSpec((None, aw, tm), col),
            pl.BlockSpec((None, None, pages, aw, PAGE_SIZE), paged),
            pl.BlockSpec((None, None, pages, aw, PAGE_SIZE), paged),
            pl.BlockSpec((tm, aw), row),
            pl.BlockSpec((None, aw, tm), col),
            pl.BlockSpec((None, tm // MOBA_BLOCK, aw), lambda i: (i, 0, 0)),
            pl.BlockSpec((tm, SGU_WIDTH), row),
        ]
        if carried:
            aliases = {len(in_specs) - 2: 1, len(in_specs) - 1: 2}
    else:
        out_shape = [
            jax.ShapeDtypeStruct((n, aw), F32),
            jax.ShapeDtypeStruct((n, aw), F32),
            jax.ShapeDtypeStruct((n, aw), F32),
            jax.ShapeDtypeStruct((n, SGU_WIDTH), BF16),
            jax.ShapeDtypeStruct((n, SGU_WIDTH), F32),
        ]
        out_specs = [pl.BlockSpec((tm, aw), row)] * 3 + [pl.BlockSpec((tm, SGU_WIDTH), row)] * 2
    return pl.pallas_call(
        functools.partial(_inproj_kernel, prompt=prompt, chunk_len=chunk_len, n_carried=len(carried)),
        out_shape=out_shape,
        grid=(n // tm,),
        in_specs=in_specs,
        out_specs=out_specs,
        input_output_aliases=aliases,
        compiler_params=_params("parallel"),
        name="inproj_prompt" if prompt else "inproj_sample",
    )(x, g, w_in, ws_tiled, bias_rows, g_sgu, *carried)


def _moba_prompt_kernel(qt_ref, kb_ref, vt_ref, km_ref, o_ref, base_ref, bias_ref, qmb_ref, sc_ref, acc_ref):
    blk = MOBA_BLOCK
    i = pl.program_id(1)
    n_blocks = km_ref.shape[0]
    pair_w = 2 * HEAD_DIM
    pair_row = lax.broadcasted_iota(jnp.int32, (pair_w, blk), 0)
    blk_idx = lax.broadcasted_iota(jnp.int32, (n_blocks, blk), 0)
    pairs = [slice((h // 2) * pair_w, (h // 2 + 1) * pair_w) for h in range(ATT_HEADS)]
    heads = [slice(h * HEAD_DIM, (h + 1) * HEAD_DIM) for h in range(ATT_HEADS)]
    slopes2 = [s * _LOG2E for s in _ALIBI_SLOPES]

    @pl.when(i == 0)
    def _():
        rk = lax.broadcasted_iota(jnp.int32, (blk, blk), 0)
        rq = lax.broadcasted_iota(jnp.int32, (blk, blk), 1)
        dist = (rq - rk).astype(F32)
        for h in range(ATT_HEADS):
            base_ref[0, h] = -slopes2[h] * dist
            base_ref[1, h] = jnp.where(rq >= rk, -slopes2[h] * dist, NEG)

    for h in range(ATT_HEADS):
        qm = jnp.where(pair_row // HEAD_DIM == h % 2, qt_ref[pairs[h], :], 0.0)
        qmb_ref[h] = (qm * _LOG2E).astype(BF16)
        gate = jnp.dot(km_ref[:, pairs[h]], qm, precision=_HIGHEST, preferred_element_type=F32)
        sel = _top_mask(gate, 0, i, blk_idx)
        bias_ref[h * n_blocks:(h + 1) * n_blocks, :] = jnp.where(
            sel | (blk_idx == i), (-slopes2[h] * blk) * (i - blk_idx).astype(F32), NEG)
    acc_ref[...] = jnp.zeros(acc_ref.shape, F32)

    def score_body(j, maxima):
        start = pl.multiple_of(j * blk, blk)
        is_own = (j == i).astype(jnp.int32)
        out = []
        for h in range(ATT_HEADS):
            s = (_dot(kb_ref[pl.ds(start, blk), pairs[h]], qmb_ref[h]) + base_ref[is_own, h]
                 + bias_ref[pl.ds(h * n_blocks + j, 1), :])
            sc_ref[h, j] = s
            out.append(jnp.maximum(maxima[h], jnp.max(s, axis=0, keepdims=True)))
        return tuple(out)

    maxima = lax.fori_loop(0, i + 1, score_body, tuple(jnp.full((1, blk), NEG, F32) for _ in range(ATT_HEADS)))

    ones_rows = jnp.ones((_DENOM_ROWS, blk), BF16)

    def value_body(j, carry):
        start = pl.multiple_of(j * blk, blk)
        for h in range(ATT_HEADS):
            p = jnp.exp2(sc_ref[h, j] - maxima[h])
            vt_ones = jnp.concatenate([vt_ref[heads[h], pl.ds(start, blk)], ones_rows], axis=0)
            acc_ref[h] += _dot(vt_ones, p.astype(BF16))
        return carry

    lax.fori_loop(0, i + 1, value_body, 0)
    outs = [acc_ref[h, 0:HEAD_DIM, :] / acc_ref[h, HEAD_DIM:HEAD_DIM + 1, :] for h in range(ATT_HEADS)]
    o_ref[...] = jnp.concatenate(outs, axis=0).T.astype(BF16)


def _moba_prompt(qt, kb, vt, kmean):
    batch, aw, t = qt.shape
    n_blocks = t // MOBA_BLOCK
    assert t % MOBA_BLOCK == 0 and kmean.shape == (batch, n_blocks, aw)
    return pl.pallas_call(
        _moba_prompt_kernel,
        out_shape=jax.ShapeDtypeStruct((batch, t, aw), BF16),
        grid=(batch, n_blocks),
        in_specs=[
            pl.BlockSpec((None, aw, MOBA_BLOCK), lambda b, i: (b, 0, i)),
            _resident((None, t, aw), lambda b, i: (b, 0, 0)),
            _resident((None, aw, t), lambda b, i: (b, 0, 0)),
            _resident((None, n_blocks, aw), lambda b, i: (b, 0, 0)),
        ],
        out_specs=pl.BlockSpec((None, MOBA_BLOCK, aw), lambda b, i: (b, i, 0)),
        scratch_shapes=[
            pltpu.VMEM((2, ATT_HEADS, MOBA_BLOCK, MOBA_BLOCK), F32),
            pltpu.VMEM((ATT_HEADS * n_blocks, MOBA_BLOCK), F32),
            pltpu.VMEM((ATT_HEADS, 2 * HEAD_DIM, MOBA_BLOCK), BF16),
            pltpu.VMEM((ATT_HEADS, n_blocks, MOBA_BLOCK, MOBA_BLOCK), F32),
            pltpu.VMEM((ATT_HEADS, HEAD_DIM + _DENOM_ROWS, MOBA_BLOCK), F32),
        ],
        compiler_params=_params("parallel", "arbitrary"),
        name="moba_prompt",
    )(qt, kb, vt, kmean)


_ROWS_PER_QUERY = 8
_MOBA_SAMPLE_REQS = 2


def _block_diag_queries(q, head_dim):
    s_len, w = q.shape
    lane_head = lax.broadcasted_iota(jnp.int32, (_ROWS_PER_QUERY, w), 1) // head_dim
    row_head = lax.broadcasted_iota(jnp.int32, (_ROWS_PER_QUERY, w), 0)
    keep = lane_head == row_head
    parts = [jnp.where(keep, jnp.broadcast_to(q[s:s + 1, :], (_ROWS_PER_QUERY, w)), 0.0) for s in range(s_len)]
    return jnp.concatenate(parts, axis=0), keep


def _pick_block_diag(r, keep, s_len):
    w = r.shape[1]
    keep_all = jnp.concatenate([keep] * s_len, axis=0)
    return jnp.sum(jnp.where(keep_all, r, 0.0).reshape(s_len, _ROWS_PER_QUERY, w), axis=1)


def _moba_sample_kernel(pt_ref, q_all, ko_all, vo_all, *rest, n_pages, past_len, reqs):
    del pt_ref
    for r in range(reqs):
        _moba_sample_request(
            q_all.at[r], ko_all.at[r], vo_all.at[r], rest[r * n_pages:(r + 1) * n_pages],
            rest[(reqs + r) * n_pages:(reqs + r + 1) * n_pages], rest[2 * reqs * n_pages].at[r],
            n_pages=n_pages, past_len=past_len)


def _moba_sample_request(q_ref, ko_ref, vo_ref, k_pages, v_pages, o_ref, *, n_pages, past_len):
    s_len, aw = q_ref.shape
    blk = MOBA_BLOCK
    pages_per_block = blk // PAGE_SIZE
    n_blocks = past_len // blk
    n_rows = s_len * _ROWS_PER_QUERY

    q = q_ref[...]
    qbd, keep = _block_diag_queries(q, HEAD_DIM)
    qbd_b = qbd.astype(BF16)
    row = lax.broadcasted_iota(jnp.int32, (n_rows, 1), 0)
    row_head = row % _ROWS_PER_QUERY
    slope = jnp.zeros((n_rows, 1), F32)
    for h in range(ATT_HEADS):
        slope = jnp.where(row_head == h, _ALIBI_SLOPES[h], slope)
    qpos = past_len + row // _ROWS_PER_QUERY

    scores = []
    blk_lane = lax.broadcasted_iota(jnp.int32, (aw, n_blocks), 1)
    kmean_t = jnp.zeros((aw, n_blocks), F32)
    for n in range(n_blocks):
        ksum = jnp.zeros((aw, PAGE_SIZE), F32)
        for t in range(pages_per_block):
            kp = k_pages[n * pages_per_block + t][...]
            ksum = ksum + kp
            scores.append(_dot(qbd_b, kp.astype(BF16)))
        kmean_t = jnp.where(blk_lane == n, jnp.sum(ksum, axis=1, keepdims=True) * (1.0 / blk), kmean_t)
    q_t = jnp.concatenate([q, jnp.zeros((_ROWS_PER_QUERY - s_len, aw), F32)], axis=0).T
    gate = jnp.concatenate(
        [jnp.sum((q_t[:, s:s + 1] * kmean_t).reshape(ATT_HEADS, HEAD_DIM, n_blocks), axis=1) for s in range(s_len)],
        axis=0)
    sel = _top_mask(gate, 1, n_blocks, lax.broadcasted_iota(jnp.int32, (n_rows, n_blocks), 1))
    unselected = jnp.where(sel, 0.0, NEG)

    key_off = lax.broadcasted_iota(jnp.int32, (n_rows, PAGE_SIZE), 1)
    for pg in range(n_pages):
        n = pg // pages_per_block
        dist = (qpos - (pg * PAGE_SIZE + key_off)).astype(F32)
        scores[pg] = scores[pg] - slope * dist + unselected[:, n:n + 1]
    k_own, v_own = ko_ref[...], vo_ref[...]
    own = []
    for t in range(s_len):
        s_t = jnp.sum(qbd * k_own[t:t + 1, :], axis=1, keepdims=True)
        d_t = qpos - (past_len + t)
        own.append(jnp.where(d_t >= 0, s_t - slope * d_t.astype(F32), NEG))

    m_all = scores[0]
    for s_pg in scores[1:]:
        m_all = jnp.maximum(m_all, s_pg)
    m = jnp.max(m_all, axis=1, keepdims=True)
    for s_t in own:
        m = jnp.maximum(m, s_t)
    l = jnp.zeros((n_rows, 1), F32)
    acc = jnp.zeros((n_rows, aw), F32)
    for t in range(s_len):
        p_t = jnp.exp(own[t] - m)
        l = l + p_t
        acc = acc + p_t * v_own[t:t + 1, :]
    p_sum = jnp.zeros((n_rows, PAGE_SIZE), F32)
    for pg in range(n_pages):
        p_pg = jnp.exp(scores[pg] - m)
        p_sum = p_sum + p_pg
        acc = acc + _dot_nt(p_pg.astype(BF16), v_pages[pg][...].astype(BF16))
    l = l + jnp.sum(p_sum, axis=1, keepdims=True)
    o_ref[...] = _pick_block_diag(acc / l, keep, s_len)


def _moba_sample(page_table, q3, k3, v3, cache_kt, cache_vt, layer):
    n_req, s_len, aw = q3.shape
    n_pages = page_table.shape[1]
    past_len = n_pages * PAGE_SIZE
    assert past_len % MOBA_BLOCK == 0 and s_len <= _ROWS_PER_QUERY and MOBA_BLOCK % PAGE_SIZE == 0
    assert past_len // MOBA_BLOCK >= MOBA_TOPK and cache_kt.shape[2:] == (aw, PAGE_SIZE)
    reqs = _MOBA_SAMPLE_REQS
    assert n_req % reqs == 0
    new = pl.BlockSpec((reqs, s_len, aw), lambda b, pt: (b, 0, 0))
    page_specs = [
        pl.BlockSpec((None, None, aw, PAGE_SIZE), lambda b, pt, r=r, p=p: (layer, pt[reqs * b + r, p], 0, 0))
        for r in range(reqs) for p in range(n_pages)
    ]
    n_ops = reqs * n_pages
    return pl.pallas_call(
        functools.partial(_moba_sample_kernel, n_pages=n_pages, past_len=past_len, reqs=reqs),
        out_shape=jax.ShapeDtypeStruct((n_req, s_len, aw), F32),
        grid_spec=pltpu.PrefetchScalarGridSpec(
            num_scalar_prefetch=1,
            grid=(n_req // reqs,),
            in_specs=[new, new, new] + page_specs + page_specs,
            out_specs=new,
        ),
        compiler_params=_params("parallel"),
        name="moba_sample",
    )(page_table, q3, k3, v3, *([cache_kt] * n_ops), *([cache_vt] * n_ops))


def _mix_out(x_ref, oatt_ref, osgu_ref, wo_ref):
    return (x_ref[...] + _dot(oatt_ref[...].astype(BF16), wo_ref[0:ATT_WIDTH, :])
            + _dot(osgu_ref[...], wo_ref[ATT_WIDTH:, :]))


def _outx_prompt_kernel(x_ref, oatt_ref, osgu_ref, wo_ref, gx_ref, wxq_ref, mk_ref, mv_ref, wxo_ref, o_ref):
    tm = x_ref.shape[0]
    x2 = _mix_out(x_ref, oatt_ref, osgu_ref, wo_ref)
    qx = _dot(_rms(x2, gx_ref[...]).astype(BF16), wxq_ref[...]) * (X_HEAD_DIM ** -0.5)
    mk = mk_ref[...].astype(BF16)
    mv = mv_ref[...].astype(BF16)
    pair_w = 2 * X_HEAD_DIM
    lane = lax.broadcasted_iota(jnp.int32, (tm, pair_w), 1)
    pieces = []
    for hp in range(X_HEADS // 2):
        pair = slice(hp * pair_w, (hp + 1) * pair_w)
        q_pair = qx[:, pair]
        res = []
        for sub in range(2):
            qm = jnp.where(lane // X_HEAD_DIM == sub, q_pair, 0.0).astype(BF16)
            s = _dot_nt(qm, mk[:, pair])
            p = jnp.exp(s - jnp.max(s, axis=1, keepdims=True))
            p = p / jnp.sum(p, axis=1, keepdims=True)
            res.append(_dot(p.astype(BF16), mv[:, pair]))
        pieces.append(jnp.where(lane // X_HEAD_DIM == 0, res[0], res[1]))
    ox = jnp.concatenate(pieces, axis=1).astype(BF16)
    o_ref[...] = x2 + _dot(ox, wxo_ref[...])


def _outx_prompt(x, oatt, osgu, layer, w_out, g_x, w_xq, mem_k, mem_v, w_xo, *, batch):
    n, d = x.shape
    tm = ROW_TILE
    tiles_per_seq = n // batch // tm
    n_mem = mem_k.shape[2]
    row = lambda i: (i, 0)
    mem = pl.BlockSpec((None, None, n_mem, X_WIDTH), lambda i: (layer, i // tiles_per_seq, 0, 0))
    return pl.pallas_call(
        _outx_prompt_kernel,
        out_shape=jax.ShapeDtypeStruct((n, d), F32),
        grid=(n // tm,),
        in_specs=[
            pl.BlockSpec((tm, d), row),
            pl.BlockSpec((tm, ATT_WIDTH), row),
            pl.BlockSpec((tm, SGU_WIDTH), row),
            _resident((None,) + w_out.shape[1:], lambda i: (layer, 0, 0)),
            _resident((None, 1, d), lambda i: (layer, 0, 0)),
            _resident((None, d, X_WIDTH), lambda i: (layer, 0, 0)),
            mem,
            mem,
            _resident((None, X_WIDTH, d), lambda i: (layer, 0, 0)),
        ],
        out_specs=pl.BlockSpec((tm, d), row),
        compiler_params=_params("parallel"),
        name="outx_prompt",
    )(x, oatt, osgu, w_out, g_x, w_xq, mem_k, mem_v, w_xo)


def _outq_sample_kernel(x_ref, oatt_ref, osgu_ref, wo_ref, gx_ref, wxq_ref, x2_ref, q_ref):
    x2 = _mix_out(x_ref, oatt_ref, osgu_ref, wo_ref)
    x2_ref[...] = x2
    q_ref[...] = _dot(_rms(x2, gx_ref[...]).astype(BF16), wxq_ref[...]) * (X_HEAD_DIM ** -0.5)


def _outq_sample(x, oatt, osgu, layer, w_out, g_x, w_xq):
    n, d = x.shape
    tm = ROW_TILE
    row = lambda i: (i, 0)
    return pl.pallas_call(
        _outq_sample_kernel,
        out_shape=[jax.ShapeDtypeStruct((n, d), F32), jax.ShapeDtypeStruct((n, X_WIDTH), F32)],
        grid=(n // tm,),
        in_specs=[
            pl.BlockSpec((tm, d), row),
            pl.BlockSpec((tm, ATT_WIDTH), row),
            pl.BlockSpec((tm, SGU_WIDTH), row),
            _resident((None,) + w_out.shape[1:], lambda i: (layer, 0, 0)),
            _resident((None, 1, d), lambda i: (layer, 0, 0)),
            _resident((None, d, X_WIDTH), lambda i: (layer, 0, 0)),
        ],
        out_specs=[pl.BlockSpec((tm, d), row), pl.BlockSpec((tm, X_WIDTH), row)],
        compiler_params=_params("parallel"),
        name="outq_sample",
    )(x, oatt, osgu, w_out, g_x, w_xq)


_XATT_REQ_TILE = 8


def _xatt_sample_kernel(q_ref, mkt_ref, mvt_ref, o_ref):
    s_len = q_ref.shape[1]
    for b in range(q_ref.shape[0]):
        qbd, keep = _block_diag_queries(q_ref[b], X_HEAD_DIM)
        s = _dot(qbd.astype(BF16), mkt_ref[b].astype(BF16))
        p = jnp.exp(s - jnp.max(s, axis=1, keepdims=True))
        p = p / jnp.sum(p, axis=1, keepdims=True)
        r = _dot_nt(p.astype(BF16), mvt_ref[b].astype(BF16))
        o_ref[b] = _pick_block_diag(r, keep, s_len)


def _xatt_sample(q3, mem_kt, mem_vt, layer):
    n_req, s_len, _ = q3.shape
    n_mem = mem_kt.shape[3]
    bt = _XATT_REQ_TILE
    assert n_req % bt == 0 and mem_kt.shape[2] == X_WIDTH
    mem = pl.BlockSpec((None, bt, X_WIDTH, n_mem), lambda i: (layer, i, 0, 0))
    blk = pl.BlockSpec((bt, s_len, X_WIDTH), lambda i: (i, 0, 0))
    return pl.pallas_call(
        _xatt_sample_kernel,
        out_shape=jax.ShapeDtypeStruct((n_req, s_len, X_WIDTH), F32),
        grid=(n_req // bt,),
        in_specs=[blk, mem, mem],
        out_specs=blk,
        compiler_params=_params("parallel"),
        name="xatt_sample",
    )(q3, mem_kt, mem_vt)


def _memkv_kernel(mem_ref, g_ref, wk_ref, wv_ref, mk_ref, mv_ref):
    hm = _rms(mem_ref[...], g_ref[...]).astype(BF16)
    mk_ref[...] = _dot(hm, wk_ref[...])
    mv_ref[...] = _dot(hm, wv_ref[...])


def _memkv(mem, g_mem, w_xk, w_xv):
    n, d = mem.shape
    depth = g_mem.shape[0]
    tm = min(ROW_TILE, n)
    assert n % tm == 0
    w = pl.BlockSpec((None, d, X_WIDTH), lambda l, i: (l, 0, 0))
    out = pl.BlockSpec((None, tm, X_WIDTH), lambda l, i: (l, i, 0))
    return pl.pallas_call(
        _memkv_kernel,
        out_shape=[jax.ShapeDtypeStruct((depth, n, X_WIDTH), F32)] * 2,
        grid=(depth, n // tm),
        in_specs=[pl.BlockSpec((tm, d), lambda l, i: (i, 0)), pl.BlockSpec((None, 1, d), lambda l, i: (l, 0, 0)), w, w],
        out_specs=[out, out],
        compiler_params=_params("parallel", "parallel"),
        name="memkv",
    )(mem, g_mem, w_xk, w_xv)


def kernel(x_prompt, x_sample, cache_k, cache_v, cache_mem_k, cache_mem_v, page_table, mem_prompt, g_ffa, w_ffa_in, w_ffa_out, g_mix, w_in, w_out, w_sgu_s, b_sgu_s, g_sgu, g_x, w_xq, g_mem, w_xk, w_xv, w_xo, g_ffb, w_ffb_in, w_ffb_out, g_final):
    bp, t, d = x_prompt.shape
    bs, s_len, _ = x_sample.shape
    depth = g_ffa.shape[0]
    n_mem = mem_prompt.shape[1]
    assert t % SGU_CHUNK == 0 and SGU_CHUNK % s_len == 0 and s_len < SGU_CHUNK

    bf = lambda w: w.astype(BF16)
    w_ffa_in, w_ffa_out, w_ffb_in, w_ffb_out = bf(w_ffa_in), bf(w_ffa_out), bf(w_ffb_in), bf(w_ffb_out)
    w_in, w_out, w_xq, w_xk, w_xv, w_xo = bf(w_in), bf(w_out), bf(w_xq), bf(w_xk), bf(w_xv), bf(w_xo)
    g3 = lambda g: g.reshape(depth, 1, g.shape[-1])
    g_ffa, g_mix, g_sgu, g_x, g_mem, g_ffb = g3(g_ffa), g3(g_mix), g3(g_sgu), g3(g_x), g3(g_mem), g3(g_ffb)
    g_final = g_final.reshape(1, d)

    reps = SGU_CHUNK // s_len
    ws_prompt = w_sgu_s
    ws_sample = jnp.tile(w_sgu_s[:, :, :s_len, :s_len], (1, 1, reps, reps))
    bias_prompt = jnp.repeat(jnp.swapaxes(b_sgu_s, 1, 2), SGU_GROUP_DIM, axis=2)
    bias_sample = jnp.tile(bias_prompt[:, :s_len], (1, reps, 1))

    pos_minor = lambda c: jnp.transpose(c, (0, 1, 3, 4, 2)).reshape(depth, c.shape[1], -1, c.shape[2])
    cache_kt, cache_vt = pos_minor(cache_k), pos_minor(cache_v)
    mem_kt, mem_vt = pos_minor(cache_mem_k), pos_minor(cache_mem_v)

    mk_p, mv_p = _memkv(mem_prompt.reshape(bp * n_mem, d), g_mem, w_xk, w_xv)
    mk_p4 = mk_p.reshape(depth, bp, n_mem, X_WIDTH)
    mv_p4 = mv_p.reshape(depth, bp, n_mem, X_WIDTH)

    xp = x_prompt.reshape(bp * t, d)
    xs = x_sample.reshape(bs * s_len, d)
    ks_l, vs_l, vrow_l = [], [], []
    kv_t = ()
    for l in range(depth):
        last = l == depth - 1
        xp = _ffn(xp, l, g_ffa, w_ffa_in, w_ffa_out)
        qt, kt_all, vt_all, kb, vt, kmean, osgu = _inproj(
            xp, l, g_mix, w_in, ws_prompt, bias_prompt, g_sgu, prompt=True, chunk_len=SGU_CHUNK, batch=bp,
            carried=kv_t)
        kv_t = (kt_all, vt_all)
        oatt = _moba_prompt(qt, kb.reshape(bp, t, ATT_WIDTH), vt, kmean.reshape(bp, t // MOBA_BLOCK, ATT_WIDTH))
        xp = _outx_prompt(xp, oatt.reshape(bp * t, ATT_WIDTH), osgu, l, w_out, g_x, w_xq, mk_p4, mv_p4, w_xo, batch=bp)
        xp = _ffn(xp, l, g_ffb, w_ffb_in, w_ffb_out, final_g=g_final if last else None)
        xs = _ffn(xs, l, g_ffa, w_ffa_in, w_ffa_out)
        q, k, v, osgu, vn = _inproj(
            xs, l, g_mix, w_in, ws_sample, bias_sample, g_sgu, prompt=False, chunk_len=s_len, batch=bs)
        to3 = lambda a: a.reshape(bs, s_len, a.shape[-1])
        ks_l.append(k)
        vs_l.append(v)
        vrow_l.append(vn)
        oatt = _moba_sample(page_table, to3(q), to3(k), to3(v), cache_kt, cache_vt, l)
        x2, qx = _outq_sample(xs, oatt.reshape(bs * s_len, ATT_WIDTH), osgu, l, w_out, g_x, w_xq)
        ox = _xatt_sample(to3(qx), mem_kt, mem_vt, l)
        xs = _ffn(x2, l, g_ffb, w_ffb_in, w_ffb_out, pre=(ox.reshape(bs * s_len, X_WIDTH), w_xo),
                  final_g=g_final if last else None)

    y_prompt = xp.reshape(bp, t, d)
    y_sample = xs.reshape(bs, s_len, d)
    paged = lambda a: jnp.transpose(
        a.reshape(depth, bp, t // PAGE_SIZE, ATT_HEADS, HEAD_DIM, PAGE_SIZE), (0, 1, 2, 5, 3, 4))
    k_prompt, v_prompt = paged(kv_t[0]), paged(kv_t[1])
    mem_k_prompt = mk_p.reshape(depth, bp, n_mem, X_HEADS, X_HEAD_DIM)
    mem_v_prompt = mv_p.reshape(depth, bp, n_mem, X_HEADS, X_HEAD_DIM)
    k_sample = jnp.stack(ks_l).reshape(depth, bs, s_len, ATT_HEADS, HEAD_DIM)
    v_sample = jnp.stack(vs_l).reshape(depth, bs, s_len, ATT_HEADS, HEAD_DIM)
    sgu_v_sample = jnp.stack(vrow_l).reshape(depth, bs, s_len, SGU_WIDTH)
    return (y_prompt, y_sample, k_prompt, v_prompt, mem_k_prompt, mem_v_prompt, k_sample, v_sample, sgu_v_sample)
```
